```python
import jax, jax.numpy as jnp
from jax import lax
import numpy as np

D_MODEL = 2048
BATCH = 2
SEQ = 4096
DEPTH = 2
DEC_BATCH = 32
DEC_SEQ = 8
PAST_LEN = 8192
PAGE_SIZE = 128

N_A = DEPTH // 2
N_B = DEPTH - N_A
HEAD_A = 64
H_A = D_MODEL // HEAD_A
HD_B = 128
H_B = D_MODEL // HD_B
D_DECAY_LORA = 96
D_AAA_LORA = 96
D_GATE_LORA = 256
D_FF = ((8 * D_MODEL // 3 + 255) // 256) * 256
CONV_WIDTH = 3
Q_BLOCK = 128
NORM_EPS = 1e-6
GN_EPS = 64e-5
ATTN_SCALE = HD_B ** -0.5

kernel_name = "yoco_rwkv7_fox_convffn_step"

F32 = jnp.float32


def rms_norm(x, g):
    xf = x.astype(F32)
    return (xf * lax.rsqrt(jnp.mean(xf * xf, axis=-1, keepdims=True) + NORM_EPS) * g.astype(F32)).astype(x.dtype)


def head_rms(x, g):
    xf = x.astype(F32)
    return (xf * lax.rsqrt(jnp.mean(xf * xf, axis=-1, keepdims=True) + NORM_EPS) * g.astype(F32)).astype(x.dtype)


def wkv7_step(S, inp):
    r_t, w_t, k_t, v_t, kk_t, a_t = inp
    sa = -jnp.einsum('bhij,bhj->bhi', S, kk_t)
    S = (S * w_t[:, :, None, :]
         + sa[..., None] * (kk_t * a_t)[:, :, None, :]
         + v_t[..., None] * k_t[:, :, None, :])
    y = jnp.einsum('bhij,bhj->bhi', S, r_t)
    return S, y


def rwkv7_time_mix(xn, shift0, S0, mix, w_rkv, w0, w1, w2, a0, a1, a2, g1, g2, k_k, k_a, r_k, lnx_w, lnx_b, w_o):
    B, T, D = xn.shape
    x_prev = jnp.concatenate([shift0[:, None].astype(xn.dtype), xn[:, :-1]], axis=1)
    xx = x_prev - xn
    xr, xw, xk, xv, xa, xg = [xn + xx * mix[c] for c in range(6)]
    r = xr @ w_rkv[0]
    k = xk @ w_rkv[1]
    v = xv @ w_rkv[2]
    w_raw = -jax.nn.softplus(-(w0 + jnp.tanh(xw @ w1) @ w2)) - 0.5
    decay = jnp.exp(-jnp.exp(w_raw.astype(F32)))
    a = jax.nn.sigmoid(a0 + (xa @ a1) @ a2)
    g = jax.nn.sigmoid(xg @ g1) @ g2

    def heads(t):
        return t.reshape(B, T, H_A, HEAD_A).astype(F32)

    kk = heads(k * k_k)
    kk = kk / jnp.maximum(jnp.sqrt(jnp.sum(kk * kk, axis=-1, keepdims=True)), 1e-12)
    k = k * (1 + (a - 1) * k_a)
    rh, wh, kh, vh, ah = heads(r), heads(decay), heads(k), heads(v), heads(a)

    def tm(t):
        return jnp.moveaxis(t, 1, 0)

    S_fin, y = lax.scan(wkv7_step, S0.astype(F32), (tm(rh), tm(wh), tm(kh), tm(vh), tm(kk), tm(ah)))
    y = jnp.moveaxis(y, 0, 1)
    mu = jnp.mean(y, axis=-1, keepdims=True)
    var = jnp.mean(jnp.square(y - mu), axis=-1, keepdims=True)
    y = ((y - mu) * lax.rsqrt(var + GN_EPS)).reshape(B, T, D) * lnx_w.astype(F32) + lnx_b.astype(F32)
    bonus = jnp.sum(rh * kh * r_k.astype(F32), axis=-1, keepdims=True) * vh
    y = y + bonus.reshape(B, T, D)
    out = (y.astype(xn.dtype) * g) @ w_o
    return out, xn[:, -1], S_fin.astype(S0.dtype)


def conv_ffn(xn, conv0, w_ug, conv_w, conv_b, w_down):
    T = xn.shape[1]
    u, gt = jnp.split(xn @ w_ug, 2, axis=-1)
    gp = jnp.concatenate([conv0.astype(gt.dtype), gt], axis=1)
    gc = conv_b + gp[:, 0:T] * conv_w[0]
    for j in range(1, CONV_WIDTH):
        gc = gc + gp[:, j:j + T] * conv_w[j]
    h = jax.nn.silu(gc) * u
    return h @ w_down, gp[:, -(CONV_WIDTH - 1):]


def shared_kv(x, kv_norm, w_kvf, b_f, k_norm):
    B, T, _ = x.shape
    kvf = rms_norm(x, kv_norm) @ w_kvf
    k = head_rms(kvf[..., :D_MODEL].reshape(B, T, H_B, HD_B), k_norm)
    v = kvf[..., D_MODEL:2 * D_MODEL].reshape(B, T, H_B, HD_B)
    logf = jax.nn.log_sigmoid(kvf[..., 2 * D_MODEL:].astype(F32) + b_f.astype(F32))
    return k, v, logf


def fox_prompt(q, k, v, logf):
    B, S, H, hd = q.shape
    c = jnp.cumsum(logf.astype(F32), axis=1)
    n_blk = S // Q_BLOCK
    qb = jnp.moveaxis(q.astype(F32).reshape(B, n_blk, Q_BLOCK, H, hd), 1, 0)
    cb = jnp.moveaxis(c.reshape(B, n_blk, Q_BLOCK, H), 1, 0)
    kf, vf = k.astype(F32), v.astype(F32)
    ck = jnp.transpose(c, (0, 2, 1))[:, :, None, :]
    kpos = jnp.arange(S)

    def block(inp):
        i, q_i, c_i = inp
        s = jnp.einsum('bqhd,bkhd->bhqk', q_i, kf) + jnp.transpose(c_i, (0, 2, 1))[..., None] - ck
        qpos = i * Q_BLOCK + jnp.arange(Q_BLOCK)
        s = jnp.where(kpos[None, :] <= qpos[:, None], s, -jnp.inf)
        p = jax.nn.softmax(s, axis=-1)
        return jnp.einsum('bhqk,bkhd->bqhd', p, vf)

    out = lax.map(block, (jnp.arange(n_blk), qb, cb))
    return jnp.moveaxis(out, 0, 1).reshape(B, S, H, hd).astype(q.dtype)


def online_update(carry, s, v):
    m, l, acc = carry
    m_new = jnp.maximum(m, jnp.max(s, axis=-1))
    alpha = jnp.exp(m - m_new)
    p = jnp.exp(s - m_new[..., None])
    l = l * alpha + jnp.sum(p, axis=-1)
    acc = acc * alpha[..., None] + jnp.einsum('bhqk,bkhd->bhqd', p, v.astype(F32))
    return m_new, l, acc


def fox_sample(q, k_new, v_new, logf_new, cache_k, cache_v, cache_logf, page_table):
    Bd, T, H, hd = q.shape
    n_pages = page_table.shape[1]
    page = cache_k.shape[1]
    logf_past = cache_logf[page_table].reshape(Bd, n_pages * page, H).astype(F32)
    c_all = jnp.cumsum(jnp.concatenate([logf_past, logf_new.astype(F32)], axis=1), axis=1)
    c_past = jnp.moveaxis(c_all[:, :n_pages * page].reshape(Bd, n_pages, page, H), 1, 0)
    c_q = c_all[:, n_pages * page:]
    cq = jnp.transpose(c_q, (0, 2, 1))[..., None]
    qf = q.astype(F32)

    def page_step(carry, inp):
        pt_j, c_j = inp
        k_j = cache_k[pt_j].astype(F32)
        s = jnp.einsum('bqhd,bkhd->bhqk', qf, k_j) + cq - jnp.transpose(c_j, (0, 2, 1))[:, :, None, :]
        return online_update(carry, s, cache_v[pt_j]), None

    init = (jnp.full((Bd, H, T), -jnp.inf, F32), jnp.zeros((Bd, H, T), F32), jnp.zeros((Bd, H, T, hd), F32))
    carry, _ = lax.scan(page_step, init, (page_table.T, c_past))
    s = jnp.einsum('bqhd,bkhd->bhqk', qf, k_new.astype(F32)) + cq - jnp.transpose(c_q, (0, 2, 1))[:, :, None, :]
    causal = jnp.arange(T)[None, :] <= jnp.arange(T)[:, None]
    s = jnp.where(causal, s, -jnp.inf)
    m, l, acc = online_update(carry, s, v_new)
    out = acc / l[..., None]
    return jnp.transpose(out, (0, 2, 1, 3)).astype(q.dtype)


def setup_inputs(seed: int = 0) -> dict:
    key = jax.random.key(seed)
    ks = iter(jax.random.split(key, 48))

    def nrm(shape, scale):
        return jax.random.normal(next(ks), shape, F32) * scale

    n_pages = PAST_LEN // PAGE_SIZE
    n_used = DEC_BATCH * n_pages
    n_pool = n_used + (n_used + 3) // 4
    page_table = jax.random.permutation(next(ks), n_pool)[:n_used].reshape(DEC_BATCH, n_pages).astype(jnp.int32)
    D = D_MODEL
    return {
        "x_prompt": nrm((BATCH, SEQ, D), 1.0),
        "x_sample": nrm((DEC_BATCH, DEC_SEQ, D), 1.0),
        "state_wkv": nrm((N_A, DEC_BATCH, H_A, HEAD_A, HEAD_A), 0.5),
        "state_shift": nrm((N_A, DEC_BATCH, D), 1.0),
        "state_conv": nrm((DEPTH, DEC_BATCH, CONV_WIDTH - 1, D_FF), 1.0),
        "cache_k": nrm((n_pool, PAGE_SIZE, H_B, HD_B), 1.0),
        "cache_v": nrm((n_pool, PAGE_SIZE, H_B, HD_B), 1.0),
        "cache_logf": jax.nn.log_sigmoid(4.0 + nrm((n_pool, PAGE_SIZE, H_B), 0.5)),
        "page_table": page_table,
        "att_norm": 1.0 + nrm((N_A, D), 0.05),
        "time_mix": jax.random.uniform(next(ks), (N_A, 6, D), F32),
        "w_rkv": nrm((N_A, 3, D, D), D ** -0.5),
        "w0": nrm((N_A, D), 0.5) - 1.0,
        "w1": nrm((N_A, D, D_DECAY_LORA), D ** -0.5),
        "w2": nrm((N_A, D_DECAY_LORA, D), 0.1 * D_DECAY_LORA ** -0.5),
        "a0": nrm((N_A, D), 0.1),
        "a1": nrm((N_A, D, D_AAA_LORA), D ** -0.5),
        "a2": nrm((N_A, D_AAA_LORA, D), 0.1 * D_AAA_LORA ** -0.5),
        "g1": nrm((N_A, D, D_GATE_LORA), D ** -0.5),
        "g2": nrm((N_A, D_GATE_LORA, D), D_GATE_LORA ** -0.5),
        "k_k": 0.85 + nrm((N_A, D), 0.05),
        "k_a": 1.0 + nrm((N_A, D), 0.05),
        "r_k": nrm((N_A, H_A, HEAD_A), 0.1),
        "lnx_w": 1.0 + nrm((N_A, D), 0.05),
        "lnx_b": nrm((N_A, D), 0.02),
        "w_o_a": nrm((N_A, D, D), D ** -0.5),
        "ffn_norm": 1.0 + nrm((DEPTH, D), 0.05),
        "w_ug": nrm((DEPTH, D, 2 * D_FF), D ** -0.5),
        "conv_w": nrm((DEPTH, CONV_WIDTH, D_FF), CONV_WIDTH ** -0.5),
        "conv_b": nrm((DEPTH, D_FF), 0.02),
        "w_down": nrm((DEPTH, D_FF, D), D_FF ** -0.5),
        "kv_norm": 1.0 + nrm((D,), 0.05),
        "w_kvf": jnp.concatenate([nrm((D, 2 * D), D ** -0.5), nrm((D, H_B), 0.1 * D ** -0.5)], axis=1),
        "b_f": 4.0 + nrm((H_B,), 0.5),
        "k_norm": 1.0 + nrm((HD_B,), 0.05),
        "attn_norm_b": 1.0 + nrm((N_B, D), 0.05),
        "w_q": nrm((N_B, D, D), D ** -0.5),
        "q_norm": 1.0 + nrm((N_B, HD_B), 0.05),
        "w_o_b": nrm((N_B, D, D), D ** -0.5),
    }


def reference(x_prompt, x_sample, state_wkv, state_shift, state_conv, cache_k, cache_v, cache_logf, page_table,
              att_norm, time_mix, w_rkv, w0, w1, w2, a0, a1, a2, g1, g2, k_k, k_a, r_k, lnx_w, lnx_b, w_o_a,
              ffn_norm, w_ug, conv_w, conv_b, w_down, kv_norm, w_kvf, b_f, k_norm, attn_norm_b, w_q, q_norm, w_o_b):

    def run(x, wkv0, shift0, conv0, attend):
        B, T, _ = x.shape
        wkv_out, shift_out, conv_out = [], [], []
        k_s = v_s = logf_s = None
        for layer in range(DEPTH):
            if layer < N_A:
                i = layer
                xn = rms_norm(x, att_norm[i])
                o, sh, S = rwkv7_time_mix(xn, shift0[i], wkv0[i], time_mix[i], w_rkv[i], w0[i], w1[i], w2[i],
                                          a0[i], a1[i], a2[i], g1[i], g2[i], k_k[i], k_a[i], r_k[i],
                                          lnx_w[i], lnx_b[i], w_o_a[i])
                x = x + o
                wkv_out.append(S)
                shift_out.append(sh)
            else:
                j = layer - N_A
                if j == 0:
                    k_s, v_s, logf_s = shared_kv(x, kv_norm, w_kvf, b_f, k_norm)
                xn = rms_norm(x, attn_norm_b[j])
                q = head_rms((xn @ w_q[j]).reshape(B, T, H_B, HD_B), q_norm[j]) * ATTN_SCALE
                o = attend(q, k_s, v_s, logf_s)
                x = x + o.reshape(B, T, D_MODEL) @ w_o_b[j]
            xn = rms_norm(x, ffn_norm[layer])
            o, cs = conv_ffn(xn, conv0[layer], w_ug[layer], conv_w[layer], conv_b[layer], w_down[layer])
            x = x + o
            conv_out.append(cs)
        return x, jnp.stack(wkv_out), jnp.stack(shift_out), jnp.stack(conv_out), k_s, v_s, logf_s

    Bp = x_prompt.shape[0]
    zeros_wkv = jnp.zeros((N_A, Bp, H_A, HEAD_A, HEAD_A), F32)
    zeros_shift = jnp.zeros((N_A, Bp, D_MODEL), x_prompt.dtype)
    zeros_conv = jnp.zeros((DEPTH, Bp, CONV_WIDTH - 1, D_FF), x_prompt.dtype)
    y_prompt, p_wkv, p_shift, p_conv, p_k, p_v, p_logf = run(x_prompt, zeros_wkv, zeros_shift, zeros_conv, fox_prompt)

    def attend_sample(q, k, v, lf):
        return fox_sample(q, k, v, lf, cache_k, cache_v, cache_logf, page_table)

    y_sample, s_wkv, s_shift, s_conv, s_k, s_v, s_logf = run(x_sample, state_wkv, state_shift, state_conv, attend_sample)
    return (y_prompt, y_sample, p_wkv, p_shift, p_conv, p_k, p_v, p_logf, s_wkv, s_shift, s_conv, s_k, s_v, s_logf)
```

```python
import functools
import math

import jax
import jax.numpy as jnp
from jax import lax
from jax.experimental import pallas as pl
from jax.experimental.pallas import tpu as pltpu

F32 = jnp.float32
BF16 = jnp.bfloat16

D_MODEL = 2048
HEAD_A = 64
H_A = D_MODEL // HEAD_A
HD_B = 128
H_B = D_MODEL // HD_B
NORM_EPS = 1e-6
GN_EPS = 64e-5
ATTN_SCALE = HD_B ** -0.5
LORA_PAD = 128
VMEM_LIMIT = 56 * 1024 * 1024
HIGHEST = lax.Precision.HIGHEST


def _params(*sem):
    return pltpu.CompilerParams(dimension_semantics=sem, vmem_limit_bytes=VMEM_LIMIT)


def _sigmoid(x):
    return 1.0 / (1.0 + jnp.exp(-x))


def _log_sigmoid(z):
    return jnp.minimum(z, 0.0) - jnp.log1p(jnp.exp(-jnp.abs(z)))


def _dot(a, b):
    return jnp.dot(a.astype(BF16), b.astype(BF16), preferred_element_type=F32)


def _dot_nt(a, b):
    return lax.dot_general(a.astype(BF16), b.astype(BF16), (((1,), (1,)), ((), ())), preferred_element_type=F32)


def _dot_tn(a, b):
    return lax.dot_general(a.astype(BF16), b.astype(BF16), (((0,), (0,)), ((), ())), preferred_element_type=F32)


def _dot_f32(a, b):
    return jnp.dot(a, b, precision=HIGHEST, preferred_element_type=F32)


def _rms_kernel(x_ref, g_ref, o_ref):
    x = x_ref[...]
    ms = jnp.mean(x * x, axis=-1, keepdims=True)
    o_ref[...] = x * lax.rsqrt(ms + NORM_EPS) * g_ref[...]


def _rmsnorm(x, g, tm):
    M, D = x.shape
    return pl.pallas_call(
        _rms_kernel,
        grid=(M // tm,),
        in_specs=[pl.BlockSpec((tm, D), lambda i: (i, 0)), pl.BlockSpec((1, D), lambda i: (0, 0))],
        out_specs=pl.BlockSpec((tm, D), lambda i: (i, 0)),
        out_shape=jax.ShapeDtypeStruct((M, D), F32),
        compiler_params=_params("parallel"),
        name="rmsnorm",
    )(x, g.reshape(1, D))


def _rkv_kernel(xn_ref, xp_ref, mix_ref, w_ref, o_ref, lhs_ref):
    @pl.when(pl.program_id(2) == 0)
    def _():
        xn = xn_ref[...]
        lhs_ref[...] = (xn + (xp_ref[...] - xn) * mix_ref[0]).astype(BF16)

    o_ref[0] = jnp.dot(lhs_ref[...], w_ref[0], preferred_element_type=F32)


def _rkv(xn, xprev, mix3, w3, tm, tn):
    M, D = xn.shape
    N = w3.shape[2]
    return pl.pallas_call(
        _rkv_kernel,
        grid=(M // tm, 3, N // tn),
        in_specs=[pl.BlockSpec((tm, D), lambda m, c, n: (m, 0)),
                  pl.BlockSpec((tm, D), lambda m, c, n: (m, 0)),
                  pl.BlockSpec((1, 1, D), lambda m, c, n: (c, 0, 0)),
                  pl.BlockSpec((1, D, tn), lambda m, c, n: (c, 0, n))],
        out_specs=pl.BlockSpec((1, tm, tn), lambda m, c, n: (c, m, n)),
        out_shape=jax.ShapeDtypeStruct((3, M, N), F32),
        scratch_shapes=[pltpu.VMEM((tm, D), BF16)],
        compiler_params=_params("parallel", "arbitrary", "arbitrary"),
        name="rkv_proj",
    )(xn, xprev, mix3.reshape(3, 1, D), w3)


def _lora_kernel(xn_ref, xp_ref, mix_ref, w1_ref, a1_ref, g1_ref, w2_ref, a2_ref, g2_ref, w0_ref, a0_ref,
                 lw_ref, a_ref, g_ref):
    xn = xn_ref[...]
    xx = xp_ref[...] - xn
    xw = xn + xx * mix_ref[0:1, :]
    xa = xn + xx * mix_ref[1:2, :]
    xg = xn + xx * mix_ref[2:3, :]
    hw = jnp.tanh(_dot(xw, w1_ref[...]))
    wv = w0_ref[...] + _dot(hw, w2_ref[...])
    w_raw = _log_sigmoid(wv) - 0.5
    lw_ref[...] = -jnp.exp(w_raw)
    ha = _dot(xa, a1_ref[...])
    a_ref[...] = _sigmoid(a0_ref[...] + _dot(ha, a2_ref[...]))
    hg = _sigmoid(_dot(xg, g1_ref[...]))
    g_ref[...] = _dot(hg, g2_ref[...])


def _lora(xn, xprev, mix3, w1, a1, g1, w2, a2, g2, w0, a0, tm):
    M, D = xn.shape
    row = lambda m: (m, 0)
    full = lambda m: (0, 0)
    specs = [pl.BlockSpec((tm, D), row), pl.BlockSpec((tm, D), row), pl.BlockSpec((3, D), full)]
    specs += [pl.BlockSpec(t.shape, full) for t in (w1, a1, g1, w2, a2, g2)]
    specs += [pl.BlockSpec((1, D), full), pl.BlockSpec((1, D), full)]
    out = jax.ShapeDtypeStruct((M, D), F32)
    return pl.pallas_call(
        _lora_kernel,
        grid=(M // tm,),
        in_specs=specs,
        out_specs=[pl.BlockSpec((tm, D), row)] * 3,
        out_shape=[out, out, out],
        compiler_params=_params("parallel"),
        name="lora_branches",
    )(xn, xprev, mix3, w1, a1, g1, w2, a2, g2, w0.reshape(1, D), a0.reshape(1, D))


def _wkv_kernel(r_ref, k_ref, v_ref, lw_ref, a_ref, g_ref, kk_ref, ka_ref, rk_ref, lnw_ref, lnb_ref, s0_ref,
                z_ref, sout_ref, S_ref, *, C, HB, nchunks):
    c = pl.program_id(2)

    @pl.when(c == 0)
    def _():
        S_ref[...] = s0_ref[0]

    rows = lax.broadcasted_iota(jnp.int32, (C, C), 0)
    cols = lax.broadcasted_iota(jnp.int32, (C, C), 1)
    incl = cols <= rows
    strict = cols < rows
    eye = (rows == cols).astype(F32)

    lw = lw_ref[...]
    cum = _dot_f32(incl.astype(F32), lw)
    cum_last = cum[C - 1:C, :]
    e_pos = jnp.exp(cum)
    e_neg = jnp.exp(-cum)
    e_prev = jnp.exp(cum - lw)
    e_tail = jnp.exp(cum_last - cum)
    g_last = jnp.exp(cum_last)

    r = r_ref[0]
    k = k_ref[0]
    v = v_ref[0]
    a = a_ref[...]
    kkf = k * kk_ref[...]
    k2 = k * (1.0 + (a - 1.0) * ka_ref[...])

    for h in range(HB):
        sl = slice(h * HEAD_A, (h + 1) * HEAD_A)
        kk_h = kkf[:, sl]
        nrm = jnp.sqrt(jnp.sum(kk_h * kk_h, axis=-1, keepdims=True))
        kk_h = kk_h / jnp.maximum(nrm, 1e-12)
        b_h = kk_h * a[:, sl]
        r_h = r[:, sl]
        k_h = k2[:, sl]
        v_h = v[:, sl]
        Qk = kk_h * e_prev[:, sl]
        Rg = r_h * e_pos[:, sl]
        Kd = k_h * e_neg[:, sl]
        Bd = b_h * e_neg[:, sl]
        Kt = k_h * e_tail[:, sl]
        Bt = b_h * e_tail[:, sl]

        Lb = jnp.where(strict, _dot_nt(Qk, Bd), 0.0)
        Lk = jnp.where(strict, _dot_nt(Qk, Kd), 0.0)
        Uk = jnp.where(incl, _dot_nt(Rg, Kd), 0.0)
        Ub = jnp.where(incl, _dot_nt(Rg, Bd), 0.0)

        P = -Lb
        T = eye + P
        for _ in range(int(math.log2(C)) - 1):
            P = _dot_f32(P, P)
            T = T + _dot_f32(T, P)

        S = S_ref[h]
        X = _dot_f32(T, _dot_nt(Qk, S) + _dot(Lk, v_h))
        Y = _dot_nt(Rg, S) + _dot(Uk, v_h) - _dot(Ub, X)
        S_ref[h] = S * g_last[:, sl] + _dot_tn(v_h, Kt) - _dot_tn(X, Bt)

        mu = jnp.mean(Y, axis=-1, keepdims=True)
        yc = Y - mu
        var = jnp.mean(yc * yc, axis=-1, keepdims=True)
        yn = yc * lax.rsqrt(var + GN_EPS) * lnw_ref[:, sl] + lnb_ref[:, sl]
        bonus = jnp.sum(r_h * k_h * rk_ref[:, sl], axis=-1, keepdims=True) * v_h
        z_ref[:, sl] = (yn + bonus) * g_ref[:, sl]

    @pl.when(c == nchunks - 1)
    def _():
        sout_ref[0] = S_ref[...]


def _wkv(rkv, lw, a, g, k_k, k_a, r_k, lnx_w, lnx_b, state, *, B, T, C, HB, row_off):
    D = D_MODEL
    W = HB * HEAD_A
    nchunks = T // C
    base = row_off // C
    tok = lambda b, hg, c: (base + b * nchunks + c, hg)
    par = lambda b, hg, c: (0, hg)
    st = lambda b, hg, c: (b, hg, 0, 0)
    in_specs = [pl.BlockSpec((1, C, W), lambda b, hg, c, i=i: (i, base + b * nchunks + c, hg)) for i in range(3)]
    in_specs += [pl.BlockSpec((C, W), tok)] * 3
    in_specs += [pl.BlockSpec((1, W), par)] * 5
    in_specs += [pl.BlockSpec((1, HB, HEAD_A, HEAD_A), st)]
    z, s_out = pl.pallas_call(
        functools.partial(_wkv_kernel, C=C, HB=HB, nchunks=nchunks),
        grid=(B, H_A // HB, nchunks),
        in_specs=in_specs,
        out_specs=[pl.BlockSpec((C, W), lambda b, hg, c: (b * nchunks + c, hg)),
                   pl.BlockSpec((1, HB, HEAD_A, HEAD_A), st)],
        out_shape=[jax.ShapeDtypeStruct((B * T, D), F32),
                   jax.ShapeDtypeStruct((B, H_A, HEAD_A, HEAD_A), F32)],
        scratch_shapes=[pltpu.VMEM((HB, HEAD_A, HEAD_A), F32)],
        compiler_params=_params("parallel", "parallel", "arbitrary"),
        name=f"wkv7_chunk{C}",
    )(rkv, rkv, rkv, lw, a, g, k_k.reshape(1, D), k_a.reshape(1, D), r_k.reshape(1, D),
      lnx_w.reshape(1, D), lnx_b.reshape(1, D), state)
    return z, s_out


def _mm_kernel(*refs, pro, epi, scale):
    it = iter(refs)
    x_ref = next(it)
    g_ref = next(it) if pro == "rms" else None
    w_ref = next(it)
    e_ref = next(it) if epi is not None else None
    o_ref = next(it)
    lhs_ref = next(it)

    @pl.when(pl.program_id(1) == 0)
    def _():
        x = x_ref[...]
        if pro == "rms":
            ms = jnp.mean(x * x, axis=-1, keepdims=True)
            x = x * lax.rsqrt(ms + NORM_EPS) * g_ref[...]
        lhs_ref[...] = x.astype(BF16)

    acc = jnp.dot(lhs_ref[...], w_ref[...], preferred_element_type=F32)
    if epi == "res":
        o_ref[...] = acc + e_ref[...]
    elif epi == "logsig":
        o_ref[...] = _log_sigmoid(acc + e_ref[...])
    elif epi == "headrms":
        for h in range(acc.shape[1] // HD_B):
            sl = slice(h * HD_B, (h + 1) * HD_B)
            t = acc[:, sl]
            ms = jnp.mean(t * t, axis=-1, keepdims=True)
            o_ref[:, sl] = t * lax.rsqrt(ms + NORM_EPS) * e_ref[...] * scale
    else:
        o_ref[...] = acc


def _mm(x, w, *, tm, tn, gain=None, epi=None, extra=None, scale=1.0, name="mm"):
    M, K = x.shape
    N = w.shape[1]
    pro = "rms" if gain is not None else "cast"
    args = [x]
    in_specs = [pl.BlockSpec((tm, K), lambda m, n: (m, 0))]
    if gain is not None:
        args.append(gain.reshape(1, K))
        in_specs.append(pl.BlockSpec((1, K), lambda m, n: (0, 0)))
    args.append(w)
    in_specs.append(pl.BlockSpec((K, tn), lambda m, n: (0, n)))
    if epi == "res":
        args.append(extra)
        in_specs.append(pl.BlockSpec((tm, tn), lambda m, n: (m, n)))
    elif epi == "logsig":
        args.append(extra.reshape(1, N))
        in_specs.append(pl.BlockSpec((1, tn), lambda m, n: (0, n)))
    elif epi == "headrms":
        args.append(extra.reshape(1, HD_B))
        in_specs.append(pl.BlockSpec((1, HD_B), lambda m, n: (0, 0)))
    return pl.pallas_call(
        functools.partial(_mm_kernel, pro=pro, epi=epi, scale=scale),
        grid=(M // tm, N // tn),
        in_specs=in_specs,
        out_specs=pl.BlockSpec((tm, tn), lambda m, n: (m, n)),
        out_shape=jax.ShapeDtypeStruct((M, N), F32),
        scratch_shapes=[pltpu.VMEM((tm, K), BF16)],
        compiler_params=_params("parallel", "arbitrary"),
        name=name,
    )(*args)


def _ffn_down_kernel(u_ref, g0_ref, g1_ref, g2_ref, cw_ref, cb_ref, w_ref, r_ref, o_ref, *, nk):
    kstep = pl.program_id(1)

    @pl.when(kstep == 0)
    def _():
        o_ref[...] = r_ref[...]

    gc = cb_ref[...] + g2_ref[...] * cw_ref[0:1, :]
    gc = gc + g1_ref[...] * cw_ref[1:2, :]
    gc = gc + g0_ref[...] * cw_ref[2:3, :]
    hid = gc * _sigmoid(gc) * u_ref[...]
    o_ref[...] += jnp.dot(hid.astype(BF16), w_ref[...], preferred_element_type=F32)


def _ffn_down(ug, gm1, gm2, conv_w, conv_b, w_down, res, *, tm, tk):
    M = ug.shape[0]
    F, D = w_down.shape
    nk = F // tk
    return pl.pallas_call(
        functools.partial(_ffn_down_kernel, nk=nk),
        grid=(M // tm, nk),
        in_specs=[pl.BlockSpec((tm, tk), lambda m, k: (m, k)),
                  pl.BlockSpec((tm, tk), lambda m, k: (m, nk + k)),
                  pl.BlockSpec((tm, tk), lambda m, k: (m, k)),
                  pl.BlockSpec((tm, tk), lambda m, k: (m, k)),
                  pl.BlockSpec((3, tk), lambda m, k: (0, k)),
                  pl.BlockSpec((1, tk), lambda m, k: (0, k)),
                  pl.BlockSpec((tk, D), lambda m, k: (k, 0)),
                  pl.BlockSpec((tm, D), lambda m, k: (m, 0))],
        out_specs=pl.BlockSpec((tm, D), lambda m, k: (m, 0)),
        out_shape=jax.ShapeDtypeStruct((M, D), F32),
        compiler_params=_params("parallel", "arbitrary"),
        name="ffn_down",
    )(ug, ug, gm1, gm2, conv_w, conv_b.reshape(1, F), w_down, res)


def _cumsum_kernel(x_ref, o_ref, carry_ref):
    @pl.when(pl.program_id(1) == 0)
    def _():
        carry_ref[...] = jnp.zeros_like(carry_ref)

    tc = x_ref.shape[0]
    rows = lax.broadcasted_iota(jnp.int32, (tc, tc), 0)
    cols = lax.broadcasted_iota(jnp.int32, (tc, tc), 1)
    cs = _dot_f32((cols <= rows).astype(F32), x_ref[...]) + carry_ref[...]
    o_ref[...] = cs
    carry_ref[...] = cs[tc - 1:tc, :]


def _cumsum_rows(x, *, B, T, tc):
    n = T // tc
    return pl.pallas_call(
        _cumsum_kernel,
        grid=(B, n),
        in_specs=[pl.BlockSpec((tc, x.shape[1]), lambda b, j: (b * n + j, 0))],
        out_specs=pl.BlockSpec((tc, x.shape[1]), lambda b, j: (b * n + j, 0)),
        out_shape=jax.ShapeDtypeStruct((B * T, x.shape[1]), F32),
        scratch_shapes=[pltpu.VMEM((1, x.shape[1]), F32)],
        compiler_params=_params("parallel", "arbitrary"),
        name="logf_cumsum",
    )(x)


def _online_update(s, v_bf, m_ref, l_ref, acc_ref, h, sl):
    m_prev = m_ref[h]
    m_new = jnp.maximum(m_prev, jnp.max(s, axis=-1, keepdims=True))
    alpha = jnp.exp(m_prev - m_new)
    p = jnp.exp(s - m_new)
    l_ref[h] = alpha * l_ref[h] + jnp.sum(p, axis=-1, keepdims=True)
    acc_ref[:, sl] = alpha * acc_ref[:, sl] + jnp.dot(p.astype(BF16), v_bf, preferred_element_type=F32)
    m_ref[h] = m_new


def _fox_prompt_kernel(q_ref, k_ref, v_ref, cq_ref, ck_ref, o_ref, m_ref, l_ref, acc_ref, *, tq):
    qi = pl.program_id(1)
    ki = pl.program_id(2)

    @pl.when(ki == 0)
    def _():
        m_ref[...] = jnp.full_like(m_ref, -jnp.inf)
        l_ref[...] = jnp.zeros_like(l_ref)
        acc_ref[...] = jnp.zeros_like(acc_ref)

    @pl.when(ki <= qi)
    def _():
        rows = lax.broadcasted_iota(jnp.int32, (tq, tq), 0)
        cols = lax.broadcasted_iota(jnp.int32, (tq, tq), 1)
        visible = jnp.logical_or(cols <= rows, ki < qi)
        for h in range(H_B):
            sl = slice(h * HD_B, (h + 1) * HD_B)
            s = _dot_nt(q_ref[:, sl], k_ref[:, sl]) + cq_ref[:, h:h + 1] - ck_ref[0, h:h + 1, :]
            s = jnp.where(visible, s, -jnp.inf)
            _online_update(s, v_ref[:, sl].astype(BF16), m_ref, l_ref, acc_ref, h, sl)

    @pl.when(ki == qi)
    def _():
        for h in range(H_B):
            sl = slice(h * HD_B, (h + 1) * HD_B)
            o_ref[:, sl] = acc_ref[:, sl] / l_ref[h]


def _fox_prompt(q, k, v, c_col, c_row, *, B, T, tq):
    D = D_MODEL
    n = T // tq
    qmap = lambda b, qi, ki: (b * n + qi, 0)
    kmap = lambda b, qi, ki: (b * n + jnp.minimum(ki, qi), 0)
    return pl.pallas_call(
        functools.partial(_fox_prompt_kernel, tq=tq),
        grid=(B, n, n),
        in_specs=[pl.BlockSpec((tq, D), qmap), pl.BlockSpec((tq, D), kmap), pl.BlockSpec((tq, D), kmap),
                  pl.BlockSpec((tq, c_col.shape[1]), qmap),
                  pl.BlockSpec((1, H_B, tq), lambda b, qi, ki: (b, 0, jnp.minimum(ki, qi)))],
        out_specs=pl.BlockSpec((tq, D), qmap),
        out_shape=jax.ShapeDtypeStruct((B * T, D), F32),
        scratch_shapes=[pltpu.VMEM((H_B, tq, 1), F32), pltpu.VMEM((H_B, tq, 1), F32), pltpu.VMEM((tq, D), F32)],
        compiler_params=_params("parallel", "parallel", "arbitrary"),
        name="fox_prompt",
    )(q, k, v, c_col, c_row)


def _paged_cumsum_kernel(pt_ref, lt_ref, lo_ref, new_ref, ct_ref, cq_ref, ccol_ref, crow_ref, *, npages):
    j = pl.program_id(1)

    @pl.when(j == 0)
    def _():
        ccol_ref[...] = jnp.zeros_like(ccol_ref)
        crow_ref[...] = jnp.zeros_like(crow_ref)

    P = lt_ref.shape[2]
    rows = lax.broadcasted_iota(jnp.int32, (P, P), 0)
    cols = lax.broadcasted_iota(jnp.int32, (P, P), 1)
    ct = _dot_f32(lt_ref[0], (rows <= cols).astype(F32)) + ccol_ref[...]
    ct_ref[0, 0] = ct
    ccol_ref[...] = ct[:, P - 1:P]
    crow_ref[...] += jnp.sum(lo_ref[0], axis=0, keepdims=True)

    @pl.when(j == npages - 1)
    def _():
        x = new_ref[:, 0:H_B]
        t_idx = lax.broadcasted_iota(jnp.int32, x.shape, 0)
        cs = jnp.zeros_like(x)
        for u in range(x.shape[0]):
            cs = cs + jnp.where(t_idx >= u, x[u:u + 1, :], 0.0)
        cq_ref[0] = crow_ref[...] + cs


def _paged_cumsum(page_table, logf_t, logf_o, logf_new, *, Bd, Tn, row_off):
    npages = page_table.shape[1]
    page = logf_t.shape[2]
    base = row_off // Tn
    grid_spec = pltpu.PrefetchScalarGridSpec(
        num_scalar_prefetch=1,
        grid=(Bd, npages),
        in_specs=[pl.BlockSpec((1, H_B, page), lambda b, j, pt: (pt[b, j], 0, 0)),
                  pl.BlockSpec((1, page, H_B), lambda b, j, pt: (pt[b, j], 0, 0)),
                  pl.BlockSpec((Tn, logf_new.shape[1]), lambda b, j, pt: (base + b, 0))],
        out_specs=[pl.BlockSpec((1, 1, H_B, page), lambda b, j, pt: (b, j, 0, 0)),
                   pl.BlockSpec((1, Tn, H_B), lambda b, j, pt: (b, 0, 0))],
        scratch_shapes=[pltpu.VMEM((H_B, 1), F32), pltpu.VMEM((1, H_B), F32)],
    )
    return pl.pallas_call(
        functools.partial(_paged_cumsum_kernel, npages=npages),
        grid_spec=grid_spec,
        out_shape=[jax.ShapeDtypeStruct((Bd, npages, H_B, page), F32),
                   jax.ShapeDtypeStruct((Bd, Tn, H_B), F32)],
        compiler_params=_params("parallel", "arbitrary"),
        name="paged_logf_cumsum",
    )(page_table, logf_t, logf_o, logf_new)


def _fox_sample_kernel(pt_ref, q_ref, cq_ref, cqt_ref, cp_ref, kn_ref, vn_ref, *rest, G, ngroups):
    k_refs = rest[:G]
    v_refs = rest[G:2 * G]
    o_ref, m_ref, l_ref, acc_ref = rest[2 * G:]
    g = pl.program_id(1)
    Tn = q_ref.shape[0]

    @pl.when(g == 0)
    def _():
        m_ref[...] = jnp.full_like(m_ref, -jnp.inf)
        l_ref[...] = jnp.zeros_like(l_ref)
        acc_ref[...] = jnp.zeros_like(acc_ref)

    for i in range(G):
        for h in range(H_B):
            sl = slice(h * HD_B, (h + 1) * HD_B)
            s = _dot_nt(q_ref[:, sl], k_refs[i][0, :, sl]) + cq_ref[0, :, h:h + 1] - cp_ref[0, i, h:h + 1, :]
            _online_update(s, v_refs[i][0, :, sl].astype(BF16), m_ref, l_ref, acc_ref, h, sl)

    @pl.when(g == ngroups - 1)
    def _():
        rows = lax.broadcasted_iota(jnp.int32, (Tn, Tn), 0)
        cols = lax.broadcasted_iota(jnp.int32, (Tn, Tn), 1)
        for h in range(H_B):
            sl = slice(h * HD_B, (h + 1) * HD_B)
            s = _dot_nt(q_ref[:, sl], kn_ref[:, sl]) + cq_ref[0, :, h:h + 1] - cqt_ref[0, h:h + 1, :]
            s = jnp.where(cols <= rows, s, -jnp.inf)
            _online_update(s, vn_ref[:, sl].astype(BF16), m_ref, l_ref, acc_ref, h, sl)
            o_ref[:, sl] = acc_ref[:, sl] / l_ref[h]


def _fox_sample(page_table, q, k, v, c_q, c_qt, c_past, cache_k, cache_v, *, Bd, Tn, G, row_off):
    D = D_MODEL
    npages = page_table.shape[1]
    page = cache_k.shape[1]
    ngroups = npages // G
    base = row_off // Tn
    tokmap = lambda b, g, pt: (base + b, 0)
    in_specs = [pl.BlockSpec((Tn, D), tokmap),
                pl.BlockSpec((1, Tn, H_B), lambda b, g, pt: (b, 0, 0)),
                pl.BlockSpec((1, H_B, Tn), lambda b, g, pt: (b, 0, 0)),
                pl.BlockSpec((1, G, H_B, page), lambda b, g, pt: (b, g, 0, 0)),
                pl.BlockSpec((Tn, D), tokmap),
                pl.BlockSpec((Tn, D), tokmap)]
    for _ in range(2):
        in_specs += [pl.BlockSpec((1, page, D), lambda b, g, pt, i=i: (pt[b, g * G + i], 0, 0)) for i in range(G)]
    grid_spec = pltpu.PrefetchScalarGridSpec(
        num_scalar_prefetch=1,
        grid=(Bd, ngroups),
        in_specs=in_specs,
        out_specs=pl.BlockSpec((Tn, D), lambda b, g, pt: (b, 0)),
        scratch_shapes=[pltpu.VMEM((H_B, Tn, 1), F32), pltpu.VMEM((H_B, Tn, 1), F32), pltpu.VMEM((Tn, D), F32)],
    )
    return pl.pallas_call(
        functools.partial(_fox_sample_kernel, G=G, ngroups=ngroups),
        grid_spec=grid_spec,
        out_shape=jax.ShapeDtypeStruct((Bd * Tn, D), F32),
        compiler_params=_params("parallel", "arbitrary"),
        name="fox_sample",
    )(page_table, q, c_q, c_qt, c_past, k, v, *([cache_k] * G), *([cache_v] * G))


def _shift_rows(seq, first, n):
    T = seq.shape[1]
    return jnp.concatenate([first, seq], axis=1)[:, :T]


def kernel(x_prompt, x_sample, state_wkv, state_shift, state_conv, cache_k, cache_v, cache_logf, page_table, att_norm, time_mix, w_rkv, w0, w1, w2, a0, a1, a2, g1, g2, k_k, k_a, r_k, lnx_w, lnx_b, w_o_a, ffn_norm, w_ug, conv_w, conv_b, w_down, kv_norm, w_kvf, b_f, k_norm, attn_norm_b, w_q, q_norm, w_o_b):
    D = D_MODEL
    Bp, Tp, _ = x_prompt.shape
    Bd, Td, _ = x_sample.shape
    Mp, Md = Bp * Tp, Bd * Td
    F = conv_b.shape[1]
    n_pool, page = cache_k.shape[0], cache_k.shape[1]
    bf = lambda t: t.astype(BF16)

    def split(t):
        return t[:Mp].reshape(Bp, Tp, -1), t[Mp:].reshape(Bd, Td, -1)

    def join(tp, ts):
        return jnp.concatenate([tp.reshape(Mp, -1), ts.reshape(Md, -1)], axis=0)

    x = join(x_prompt, x_sample)

    xn = _rmsnorm(x, att_norm[0], tm=384)
    xn_p, xn_s = split(xn)
    xprev = join(_shift_rows(xn_p, jnp.zeros((Bp, 1, D), F32), 1), _shift_rows(xn_s, state_shift[0][:, None], 1))
    mix = time_mix[0]
    rkv = _rkv(xn, xprev, mix[jnp.array([0, 2, 3])], bf(w_rkv[0]), tm=384, tn=512)
    pad_c = lambda t: jnp.pad(t, ((0, 0), (0, LORA_PAD - t.shape[1])))
    pad_r = lambda t: jnp.pad(t, ((0, LORA_PAD - t.shape[0]), (0, 0)))
    lw, a, g = _lora(xn, xprev, mix[jnp.array([1, 4, 5])], bf(pad_c(w1[0])), bf(pad_c(a1[0])), bf(g1[0]),
                     bf(pad_r(w2[0])), bf(pad_r(a2[0])), bf(g2[0]), w0[0], a0[0], tm=384)
    wkv_args = (rkv, lw, a, g, k_k[0], k_a[0], r_k[0], lnx_w[0], lnx_b[0])
    z_p, p_wkv = _wkv(*wkv_args, jnp.zeros((Bp, H_A, HEAD_A, HEAD_A), F32), B=Bp, T=Tp, C=64, HB=8, row_off=0)
    z_s, s_wkv = _wkv(*wkv_args, state_wkv[0], B=Bd, T=Td, C=Td, HB=8, row_off=Mp)
    x = _mm(join(z_p, z_s), bf(w_o_a[0]), tm=768, tn=512, epi="res", extra=x, name="wkv_out_proj")

    def conv_ffn(x, layer, conv0_s):
        ug = _mm(x, bf(w_ug[layer]), tm=768, tn=1024, gain=ffn_norm[layer], name="ffn_up")
        gt_p, gt_s = split(ug[:, F:])
        zeros2 = jnp.zeros((Bp, 2, F), F32)
        gm1 = join(_shift_rows(gt_p, zeros2[:, 1:], 1), _shift_rows(gt_s, conv0_s[:, 1:], 1))
        gm2 = join(_shift_rows(gt_p, zeros2, 2), _shift_rows(gt_s, conv0_s, 2))
        y = _ffn_down(ug, gm1, gm2, conv_w[layer], conv_b[layer], bf(w_down[layer]), x, tm=384, tk=512)
        return y, gt_p[:, -2:], gt_s[:, -2:]

    x, p_conv0, s_conv0 = conv_ffn(x, 0, state_conv[0])

    k_all = _mm(x, bf(w_kvf[:, :D]), tm=768, tn=512, gain=kv_norm, epi="headrms", extra=k_norm, name="k_proj")
    v_all = _mm(x, bf(w_kvf[:, D:2 * D]), tm=768, tn=512, gain=kv_norm, name="v_proj")
    w_f = jnp.pad(w_kvf[:, 2 * D:], ((0, 0), (0, HD_B - H_B)))
    logf_all = _mm(x, bf(w_f), tm=768, tn=HD_B, gain=kv_norm, epi="logsig", extra=jnp.pad(b_f, (0, HD_B - H_B)),
                   name="logf_proj")

    q_all = _mm(x, bf(w_q[0]), tm=768, tn=512, gain=attn_norm_b[0], epi="headrms", extra=q_norm[0],
                scale=ATTN_SCALE, name="q_proj")
    c_col = _cumsum_rows(logf_all, B=Bp, T=Tp, tc=512)
    c_row = jnp.transpose(c_col.reshape(Bp, Tp, HD_B)[:, :, :H_B], (0, 2, 1))
    o_p = _fox_prompt(q_all, k_all, v_all, c_col, c_row, B=Bp, T=Tp, tq=256)

    c_past, c_q = _paged_cumsum(page_table, jnp.transpose(cache_logf, (0, 2, 1)), cache_logf, logf_all,
                                Bd=Bd, Tn=Td, row_off=Mp)
    o_s = _fox_sample(page_table, q_all, k_all, v_all, c_q, jnp.transpose(c_q, (0, 2, 1)), c_past,
                      cache_k.reshape(n_pool, page, D), cache_v.reshape(n_pool, page, D),
                      Bd=Bd, Tn=Td, G=4, row_off=Mp)
    x = _mm(join(o_p, o_s), bf(w_o_b[0]), tm=768, tn=512, epi="res", extra=x, name="attn_out_proj")

    x, p_conv1, s_conv1 = conv_ffn(x, 1, state_conv[1])

    y_p, y_s = split(x)
    k_p, k_s = split(k_all)
    v_p, v_s = split(v_all)
    lf_p, lf_s = split(logf_all[:, :H_B])
    heads = lambda t: t.reshape(t.shape[0], t.shape[1], H_B, HD_B)
    return (y_p, y_s,
            p_wkv[None], xn_p[:, -1][None], jnp.stack([p_conv0, p_conv1]), heads(k_p), heads(v_p), lf_p,
            s_wkv[None], xn_s[:, -1][None], jnp.stack([s_conv0, s_conv1]), heads(k_s), heads(v_s), lf_s)
```

```python
import functools
import math

import jax
import jax.numpy as jnp
from jax import lax
from jax.experimental import pallas as pl
from jax.experimental.pallas import tpu as pltpu

F32 = jnp.float32
BF16 = jnp.bfloat16

D_MODEL = 2048
HEAD_A = 64
H_A = D_MODEL // HEAD_A
HD_B = 128
H_B = D_MODEL // HD_B
NORM_EPS = 1e-6
GN_EPS = 64e-5
ATTN_SCALE = HD_B ** -0.5
LORA_PAD = 128
VMEM_LIMIT = 56 * 1024 * 1024


def _params(*sem):
    return pltpu.CompilerParams(dimension_semantics=sem, vmem_limit_bytes=VMEM_LIMIT)


def _sigmoid(x):
    return 1.0 / (1.0 + jnp.exp(-x))


def _log_sigmoid(z):
    return jnp.minimum(z, 0.0) - jnp.log1p(jnp.exp(-jnp.abs(z)))


def _dot(a, b):
    return jnp.dot(a.astype(BF16), b.astype(BF16), preferred_element_type=F32)


def _dot_nt(a, b):
    return lax.dot_general(a.astype(BF16), b.astype(BF16), (((1,), (1,)), ((), ())), preferred_element_type=F32)


def _dot_tn(a, b):
    return lax.dot_general(a.astype(BF16), b.astype(BF16), (((0,), (0,)), ((), ())), preferred_element_type=F32)


def _split2(x):
    hi = x.astype(BF16)
    return hi, (x - hi.astype(F32)).astype(BF16)


def _split3(x):
    hi = x.astype(BF16)
    r1 = x - hi.astype(F32)
    mid = r1.astype(BF16)
    return hi, mid, (r1 - mid.astype(F32)).astype(BF16)


def _dot3(a, b):
    ah, al = _split2(a)
    bh, bl = _split2(b)
    d = lambda x, y: jnp.dot(x, y, preferred_element_type=F32)
    return d(ah, bh) + (d(ah, bl) + d(al, bh))


def _dot_sel_l(sel, x):
    hi, mid, lo = _split3(x)
    d = lambda p: jnp.dot(sel, p, preferred_element_type=F32)
    return d(hi) + (d(mid) + d(lo))


def _dot_sel_r(x, sel):
    hi, mid, lo = _split3(x)
    d = lambda p: jnp.dot(p, sel, preferred_element_type=F32)
    return d(hi) + (d(mid) + d(lo))


def _rms_kernel(x_ref, g_ref, o_ref):
    x = x_ref[...]
    ms = jnp.mean(x * x, axis=-1, keepdims=True)
    o_ref[...] = x * lax.rsqrt(ms + NORM_EPS) * g_ref[...]


def _rmsnorm(x, g, tm):
    M, D = x.shape
    return pl.pallas_call(
        _rms_kernel,
        grid=(M // tm,),
        in_specs=[pl.BlockSpec((tm, D), lambda i: (i, 0)), pl.BlockSpec((1, D), lambda i: (0, 0))],
        out_specs=pl.BlockSpec((tm, D), lambda i: (i, 0)),
        out_shape=jax.ShapeDtypeStruct((M, D), F32),
        compiler_params=_params("parallel"),
        name="rmsnorm",
    )(x, g.reshape(1, D))


def _rkv_kernel(xn_ref, xp_ref, mix_ref, w_ref, o_ref, lhs_ref):
    @pl.when(pl.program_id(2) == 0)
    def _():
        xn = xn_ref[...]
        lhs_ref[...] = (xn + (xp_ref[...] - xn) * mix_ref[0]).astype(BF16)

    o_ref[0] = jnp.dot(lhs_ref[...], w_ref[0], preferred_element_type=F32)


def _rkv(xn, xprev, mix3, w3, tm, tn):
    M, D = xn.shape
    N = w3.shape[2]
    return pl.pallas_call(
        _rkv_kernel,
        grid=(M // tm, 3, N // tn),
        in_specs=[pl.BlockSpec((tm, D), lambda m, c, n: (m, 0)),
                  pl.BlockSpec((tm, D), lambda m, c, n: (m, 0)),
                  pl.BlockSpec((1, 1, D), lambda m, c, n: (c, 0, 0)),
                  pl.BlockSpec((1, D, tn), lambda m, c, n: (c, 0, n))],
        out_specs=pl.BlockSpec((1, tm, tn), lambda m, c, n: (c, m, n)),
        out_shape=jax.ShapeDtypeStruct((3, M, N), F32),
        scratch_shapes=[pltpu.VMEM((tm, D), BF16)],
        compiler_params=_params("parallel", "arbitrary", "arbitrary"),
        name="rkv_proj",
    )(xn, xprev, mix3.reshape(3, 1, D), w3)


def _lora_kernel(xn_ref, xp_ref, mix_ref, w1_ref, a1_ref, g1_ref, w2_ref, a2_ref, g2_ref, w0_ref, a0_ref,
                 lw_ref, a_ref, g_ref):
    xn = xn_ref[...]
    xx = xp_ref[...] - xn
    xw = xn + xx * mix_ref[0:1, :]
    xa = xn + xx * mix_ref[1:2, :]
    xg = xn + xx * mix_ref[2:3, :]
    hw = jnp.tanh(_dot(xw, w1_ref[...]))
    wv = w0_ref[...] + _dot(hw, w2_ref[...])
    w_raw = _log_sigmoid(wv) - 0.5
    lw_ref[...] = -jnp.exp(w_raw)
    ha = _dot(xa, a1_ref[...])
    a_ref[...] = _sigmoid(a0_ref[...] + _dot(ha, a2_ref[...]))
    hg = _sigmoid(_dot(xg, g1_ref[...]))
    g_ref[...] = _dot(hg, g2_ref[...])


def _lora(xn, xprev, mix3, w1, a1, g1, w2, a2, g2, w0, a0, tm):
    M, D = xn.shape
    row = lambda m: (m, 0)
    full = lambda m: (0, 0)
    specs = [pl.BlockSpec((tm, D), row), pl.BlockSpec((tm, D), row), pl.BlockSpec((3, D), full)]
    specs += [pl.BlockSpec(t.shape, full) for t in (w1, a1, g1, w2, a2, g2)]
    specs += [pl.BlockSpec((1, D), full), pl.BlockSpec((1, D), full)]
    out = jax.ShapeDtypeStruct((M, D), F32)
    return pl.pallas_call(
        _lora_kernel,
        grid=(M // tm,),
        in_specs=specs,
        out_specs=[pl.BlockSpec((tm, D), row)] * 3,
        out_shape=[out, out, out],
        compiler_params=_params("parallel"),
        name="lora_branches",
    )(xn, xprev, mix3, w1, a1, g1, w2, a2, g2, w0.reshape(1, D), a0.reshape(1, D))


def _wkv_kernel(r_ref, k_ref, v_ref, lw_ref, a_ref, g_ref, kk_ref, ka_ref, rk_ref, lnw_ref, lnb_ref, s0_ref,
                z_ref, sout_ref, S_ref, *, C, HB, nchunks):
    c = pl.program_id(2)

    @pl.when(c == 0)
    def _():
        S_ref[...] = s0_ref[0]

    rows = lax.broadcasted_iota(jnp.int32, (C, C), 0)
    cols = lax.broadcasted_iota(jnp.int32, (C, C), 1)
    incl = cols <= rows
    strict = cols < rows
    eye = (rows == cols).astype(F32)

    lw = lw_ref[...]
    cum = _dot_sel_l(incl.astype(BF16), lw)
    cum_last = cum[C - 1:C, :]
    e_pos = jnp.exp(cum)
    e_neg = jnp.exp(-cum)
    e_prev = jnp.exp(cum - lw)
    e_tail = jnp.exp(cum_last - cum)
    g_last = jnp.exp(cum_last)

    r = r_ref[0]
    k = k_ref[0]
    v = v_ref[0]
    a = a_ref[...]
    kkf = k * kk_ref[...]
    k2 = k * (1.0 + (a - 1.0) * ka_ref[...])

    heads = range(HB)
    sls = [slice(h * HEAD_A, (h + 1) * HEAD_A) for h in heads]
    cat = lambda x, y: jnp.concatenate([x, y], axis=0)

    kk_l, b_l = [], []
    for sl in sls:
        kk_h = kkf[:, sl]
        nrm = jnp.sqrt(jnp.sum(kk_h * kk_h, axis=-1, keepdims=True))
        kk_h = kk_h / jnp.maximum(nrm, 1e-12)
        kk_l.append(kk_h)
        b_l.append(kk_h * a[:, sl])
    QR = [cat(kk_l[h] * e_prev[:, sls[h]], r[:, sls[h]] * e_pos[:, sls[h]]) for h in heads]
    Kd = [k2[:, sl] * e_neg[:, sl] for sl in sls]
    Bd = [b_l[h] * e_neg[:, sls[h]] for h in heads]
    sk = [_dot_nt(QR[h], Kd[h]) for h in heads]
    sb = [_dot_nt(QR[h], Bd[h]) for h in heads]
    LU = [cat(jnp.where(strict, sk[h][:C], 0.0), jnp.where(incl, sk[h][C:], 0.0)) for h in heads]
    Ub = [jnp.where(incl, sb[h][C:], 0.0) for h in heads]

    Q = [-jnp.where(strict, sb[h][:C], 0.0) for h in heads]
    T = [eye + Q[h] for h in heads]
    levels = int(math.log2(C))
    if levels > 1:
        Q = [_dot3(Q[h], Q[h]) for h in heads]
    for lvl in range(1, levels):
        if lvl < levels - 1:
            R = [_dot3(cat(T[h], Q[h]), Q[h]) for h in heads]
            T = [T[h] + R[h][:C] for h in heads]
            Q = [R[h][C:] for h in heads]
        else:
            T = [T[h] + _dot3(T[h], Q[h]) for h in heads]

    S = [S_ref[h] for h in heads]
    SQ = [_dot_nt(QR[h], S[h]) for h in heads]
    LV = [_dot(LU[h], v[:, sls[h]]) for h in heads]
    X = [_dot3(T[h], SQ[h][:C] + LV[h][:C]) for h in heads]
    Y = [SQ[h][C:] + LV[h][C:] - _dot(Ub[h], X[h]) for h in heads]
    for h in heads:
        sl = sls[h]
        KBt = cat(k2[:, sl] * e_tail[:, sl], -(b_l[h] * e_tail[:, sl]))
        S_ref[h] = S[h] * g_last[:, sl] + _dot_tn(cat(v[:, sl], X[h]), KBt)

    for h in heads:
        sl = sls[h]
        mu = jnp.mean(Y[h], axis=-1, keepdims=True)
        yc = Y[h] - mu
        var = jnp.mean(yc * yc, axis=-1, keepdims=True)
        yn = yc * lax.rsqrt(var + GN_EPS) * lnw_ref[:, sl] + lnb_ref[:, sl]
        bonus = jnp.sum(r[:, sl] * k2[:, sl] * rk_ref[:, sl], axis=-1, keepdims=True) * v[:, sl]
        z_ref[:, sl] = (yn + bonus) * g_ref[:, sl]

    @pl.when(c == nchunks - 1)
    def _():
        sout_ref[0] = S_ref[...]


def _wkv(rkv, lw, a, g, k_k, k_a, r_k, lnx_w, lnx_b, state, *, B, T, C, HB, row_off):
    D = D_MODEL
    W = HB * HEAD_A
    nchunks = T // C
    base = row_off // C
    tok = lambda b, hg, c: (base + b * nchunks + c, hg)
    par = lambda b, hg, c: (0, hg)
    st = lambda b, hg, c: (b, hg, 0, 0)
    in_specs = [pl.BlockSpec((1, C, W), lambda b, hg, c, i=i: (i, base + b * nchunks + c, hg)) for i in range(3)]
    in_specs += [pl.BlockSpec((C, W), tok)] * 3
    in_specs += [pl.BlockSpec((1, W), par)] * 5
    in_specs += [pl.BlockSpec((1, HB, HEAD_A, HEAD_A), st)]
    z, s_out = pl.pallas_call(
        functools.partial(_wkv_kernel, C=C, HB=HB, nchunks=nchunks),
        grid=(B, H_A // HB, nchunks),
        in_specs=in_specs,
        out_specs=[pl.BlockSpec((C, W), lambda b, hg, c: (b * nchunks + c, hg)),
                   pl.BlockSpec((1, HB, HEAD_A, HEAD_A), st)],
        out_shape=[jax.ShapeDtypeStruct((B * T, D), F32),
                   jax.ShapeDtypeStruct((B, H_A, HEAD_A, HEAD_A), F32)],
        scratch_shapes=[pltpu.VMEM((HB, HEAD_A, HEAD_A), F32)],
        compiler_params=_params("parallel", "parallel", "arbitrary"),
        name=f"wkv7_chunk{C}",
    )(rkv, rkv, rkv, lw, a, g, k_k.reshape(1, D), k_a.reshape(1, D), r_k.reshape(1, D),
      lnx_w.reshape(1, D), lnx_b.reshape(1, D), state)
    return z, s_out


def _mm_kernel(*refs, pro, epi, scale):
    it = iter(refs)
    x_ref = next(it)
    g_ref = next(it) if pro == "rms" else None
    w_ref = next(it)
    e_ref = next(it) if epi is not None else None
    o_ref = next(it)
    lhs_ref = next(it)

    @pl.when(pl.program_id(1) == 0)
    def _():
        x = x_ref[...]
        if pro == "rms":
            ms = jnp.mean(x * x, axis=-1, keepdims=True)
            x = x * lax.rsqrt(ms + NORM_EPS) * g_ref[...]
        lhs_ref[...] = x.astype(BF16)

    acc = jnp.dot(lhs_ref[...], w_ref[...], preferred_element_type=F32)
    if epi == "res":
        o_ref[...] = acc + e_ref[...]
    elif epi == "logsig":
        o_ref[...] = _log_sigmoid(acc + e_ref[...])
    elif epi == "headrms":
        for h in range(acc.shape[1] // HD_B):
            sl = slice(h * HD_B, (h + 1) * HD_B)
            t = acc[:, sl]
            ms = jnp.mean(t * t, axis=-1, keepdims=True)
            o_ref[:, sl] = t * lax.rsqrt(ms + NORM_EPS) * e_ref[...] * scale
    else:
        o_ref[...] = acc


def _mm(x, w, *, tm, tn, gain=None, epi=None, extra=None, scale=1.0, name="mm"):
    M, K = x.shape
    N = w.shape[1]
    pro = "rms" if gain is not None else "cast"
    args = [x]
    in_specs = [pl.BlockSpec((tm, K), lambda m, n: (m, 0))]
    if gain is not None:
        args.append(gain.reshape(1, K))
        in_specs.append(pl.BlockSpec((1, K), lambda m, n: (0, 0)))
    args.append(w)
    in_specs.append(pl.BlockSpec((K, tn), lambda m, n: (0, n)))
    if epi == "res":
        args.append(extra)
        in_specs.append(pl.BlockSpec((tm, tn), lambda m, n: (m, n)))
    elif epi == "logsig":
        args.append(extra.reshape(1, N))
        in_specs.append(pl.BlockSpec((1, tn), lambda m, n: (0, n)))
    elif epi == "headrms":
        args.append(extra.reshape(1, HD_B))
        in_specs.append(pl.BlockSpec((1, HD_B), lambda m, n: (0, 0)))
    return pl.pallas_call(
        functools.partial(_mm_kernel, pro=pro, epi=epi, scale=scale),
        grid=(M // tm, N // tn),
        in_specs=in_specs,
        out_specs=pl.BlockSpec((tm, tn), lambda m, n: (m, n)),
        out_shape=jax.ShapeDtypeStruct((M, N), F32),
        scratch_shapes=[pltpu.VMEM((tm, K), BF16)],
        compiler_params=_params("parallel", "arbitrary"),
        name=name,
    )(*args)


def _ffn_down_kernel(u_ref, g0_ref, g1_ref, g2_ref, cw_ref, cb_ref, w_ref, r_ref, o_ref, *, nk):
    kstep = pl.program_id(1)

    @pl.when(kstep == 0)
    def _():
        o_ref[...] = r_ref[...]

    gc = cb_ref[...] + g2_ref[...] * cw_ref[0:1, :]
    gc = gc + g1_ref[...] * cw_ref[1:2, :]
    gc = gc + g0_ref[...] * cw_ref[2:3, :]
    hid = gc * _sigmoid(gc) * u_ref[...]
    o_ref[...] += jnp.dot(hid.astype(BF16), w_ref[...], preferred_element_type=F32)


def _ffn_down(ug, gm1, gm2, conv_w, conv_b, w_down, res, *, tm, tk):
    M = ug.shape[0]
    F, D = w_down.shape
    nk = F // tk
    return pl.pallas_call(
        functools.partial(_ffn_down_kernel, nk=nk),
        grid=(M // tm, nk),
        in_specs=[pl.BlockSpec((tm, tk), lambda m, k: (m, k)),
                  pl.BlockSpec((tm, tk), lambda m, k: (m, nk + k)),
                  pl.BlockSpec((tm, tk), lambda m, k: (m, k)),
                  pl.BlockSpec((tm, tk), lambda m, k: (m, k)),
                  pl.BlockSpec((3, tk), lambda m, k: (0, k)),
                  pl.BlockSpec((1, tk), lambda m, k: (0, k)),
                  pl.BlockSpec((tk, D), lambda m, k: (k, 0)),
                  pl.BlockSpec((tm, D), lambda m, k: (m, 0))],
        out_specs=pl.BlockSpec((tm, D), lambda m, k: (m, 0)),
        out_shape=jax.ShapeDtypeStruct((M, D), F32),
        compiler_params=_params("parallel", "arbitrary"),
        name="ffn_down",
    )(ug, ug, gm1, gm2, conv_w, conv_b.reshape(1, F), w_down, res)


def _cumsum_kernel(x_ref, o_ref, carry_ref):
    @pl.when(pl.program_id(1) == 0)
    def _():
        carry_ref[...] = jnp.zeros_like(carry_ref)

    tc = x_ref.shape[0]
    rows = lax.broadcasted_iota(jnp.int32, (tc, tc), 0)
    cols = lax.broadcasted_iota(jnp.int32, (tc, tc), 1)
    cs = _dot_sel_l((cols <= rows).astype(BF16), x_ref[...]) + carry_ref[...]
    o_ref[...] = cs
    carry_ref[...] = cs[tc - 1:tc, :]


def _cumsum_rows(x, *, B, T, tc):
    n = T // tc
    return pl.pallas_call(
        _cumsum_kernel,
        grid=(B, n),
        in_specs=[pl.BlockSpec((tc, x.shape[1]), lambda b, j: (b * n + j, 0))],
        out_specs=pl.BlockSpec((tc, x.shape[1]), lambda b, j: (b * n + j, 0)),
        out_shape=jax.ShapeDtypeStruct((B * T, x.shape[1]), F32),
        scratch_shapes=[pltpu.VMEM((1, x.shape[1]), F32)],
        compiler_params=_params("parallel", "arbitrary"),
        name="logf_cumsum",
    )(x)


def _online_update(s, v_bf, m_ref, l_ref, acc_ref, h, sl):
    m_prev = m_ref[h]
    m_new = jnp.maximum(m_prev, jnp.max(s, axis=-1, keepdims=True))
    alpha = jnp.exp(m_prev - m_new)
    p = jnp.exp(s - m_new)
    l_ref[h] = alpha * l_ref[h] + jnp.sum(p, axis=-1, keepdims=True)
    acc_ref[:, sl] = alpha * acc_ref[:, sl] + jnp.dot(p.astype(BF16), v_bf, preferred_element_type=F32)
    m_ref[h] = m_new


def _fox_prompt_kernel(q_ref, k_ref, v_ref, cq_ref, ck_ref, o_ref, m_ref, l_ref, acc_ref, *, tq, tk):
    qi = pl.program_id(1)
    kj = pl.program_id(2)
    last = (qi * tq + tq - 1) // tk

    @pl.when(kj == 0)
    def _():
        m_ref[...] = jnp.full_like(m_ref, -jnp.inf)
        l_ref[...] = jnp.zeros_like(l_ref)
        acc_ref[...] = jnp.zeros_like(acc_ref)

    def block(masked):
        if masked:
            rows = qi * tq + lax.broadcasted_iota(jnp.int32, (tq, tk), 0)
            cols = kj * tk + lax.broadcasted_iota(jnp.int32, (tq, tk), 1)
            visible = cols <= rows
        for h in range(H_B):
            sl = slice(h * HD_B, (h + 1) * HD_B)
            s = _dot_nt(q_ref[:, sl], k_ref[:, sl]) + cq_ref[:, h:h + 1] - ck_ref[0, h:h + 1, :]
            if masked:
                s = jnp.where(visible, s, -jnp.inf)
            _online_update(s, v_ref[:, sl].astype(BF16), m_ref, l_ref, acc_ref, h, sl)

    @pl.when(kj < last)
    def _():
        block(False)

    @pl.when(kj == last)
    def _():
        block(True)
        for h in range(H_B):
            sl = slice(h * HD_B, (h + 1) * HD_B)
            o_ref[:, sl] = acc_ref[:, sl] / l_ref[h]


def _fox_prompt(q, k, v, c_col, c_row, *, B, T, tq, tk):
    D = D_MODEL
    nq, nk = T // tq, T // tk
    qmap = lambda b, qi, kj: (b * nq + qi, 0)
    kblk = lambda qi, kj: jnp.minimum(kj, (qi * tq + tq - 1) // tk)
    kmap = lambda b, qi, kj: (b * nk + kblk(qi, kj), 0)
    return pl.pallas_call(
        functools.partial(_fox_prompt_kernel, tq=tq, tk=tk),
        grid=(B, nq, nk),
        in_specs=[pl.BlockSpec((tq, D), qmap), pl.BlockSpec((tk, D), kmap), pl.BlockSpec((tk, D), kmap),
                  pl.BlockSpec((tq, c_col.shape[1]), qmap),
                  pl.BlockSpec((1, H_B, tk), lambda b, qi, kj: (b, 0, kblk(qi, kj)))],
        out_specs=pl.BlockSpec((tq, D), qmap),
        out_shape=jax.ShapeDtypeStruct((B * T, D), F32),
        scratch_shapes=[pltpu.VMEM((H_B, tq, 1), F32), pltpu.VMEM((H_B, tq, 1), F32), pltpu.VMEM((tq, D), F32)],
        compiler_params=_params("parallel", "parallel", "arbitrary"),
        name="fox_prompt",
    )(q, k, v, c_col, c_row)


def _accumulate(s_tiles, v_of_head, m_ref, l_ref, acc_ref, Tn):
    m_prev = m_ref[...]
    m_new = jnp.maximum(m_prev, jnp.max(functools.reduce(jnp.maximum, s_tiles), axis=1, keepdims=True))
    alpha = jnp.exp(m_prev - m_new)
    p = [jnp.exp(t - m_new) for t in s_tiles]
    l_ref[...] = alpha * l_ref[...] + jnp.sum(functools.reduce(jnp.add, p), axis=1, keepdims=True)
    m_ref[...] = m_new
    pv = []
    for h in range(H_B):
        lhs = jnp.concatenate([t[h * Tn:(h + 1) * Tn, :] for t in p], axis=1).astype(BF16)
        pv.append(jnp.dot(lhs, v_of_head(h), preferred_element_type=F32))
    acc_ref[...] = alpha * acc_ref[...] + jnp.concatenate(pv, axis=0)


def _fox_sample_kernel(pt_ref, q_ref, lfn_ref, kn_ref, vn_ref, *rest, G, ngroups):
    k_refs, v_refs, lf_refs = rest[:G], rest[G:2 * G], rest[2 * G:3 * G]
    o_ref, qbd_ref, kbf_ref, vbf_ref, m_ref, l_ref, acc_ref, carry_ref, cq_ref, xn_ref = rest[3 * G:]
    g = pl.program_id(1)
    Tn, D = q_ref.shape
    P = k_refs[0].shape[1] // H_B
    HT = H_B * Tn
    iota = lambda shape, d: lax.broadcasted_iota(jnp.int32, shape, d)
    hsl = lambda h: slice(h * HD_B, (h + 1) * HD_B)

    @pl.when(g == 0)
    def _():
        q = q_ref[...]
        lane_head = iota((Tn, D), 1) // HD_B
        for hp in range(H_B // 2):
            pair = jnp.concatenate([jnp.where(lane_head == 2 * hp, q, 0.0),
                                    jnp.where(lane_head == 2 * hp + 1, q, 0.0)], axis=0)
            qbd_ref[hp * 2 * Tn:(hp + 1) * 2 * Tn, :] = pair.astype(BF16)
        x = lfn_ref[...]
        t_idx = iota(x.shape, 0)
        cs = jnp.zeros_like(x)
        for u in range(Tn):
            cs = cs + jnp.where(t_idx >= u, x[u:u + 1, :], 0.0)
        spread = (iota((x.shape[1], HT), 0) == iota((x.shape[1], HT), 1) // Tn).astype(BF16)
        xn = _dot_sel_r(cs, spread)
        xn_ref[...] = xn
        own = iota((Tn, HT), 0) == iota((Tn, HT), 1) % Tn
        cq_ref[...] = jnp.sum(jnp.where(own, xn, 0.0), axis=0, keepdims=True)
        m_ref[...] = jnp.full_like(m_ref, -jnp.inf)
        l_ref[...] = jnp.zeros_like(l_ref)
        acc_ref[...] = jnp.zeros_like(acc_ref)
        carry_ref[...] = jnp.zeros_like(carry_ref)

    after = (iota((P, P), 0) > iota((P, P), 1)).astype(BF16)
    rows_of_head = (iota((HT, H_B), 0) // Tn == iota((HT, H_B), 1)).astype(BF16)
    carry = carry_ref[...]
    bias = []
    for i in range(G):
        lf = lf_refs[i][0]
        suf = _dot_sel_r(lf, after) + carry
        carry = suf[:, 0:1] + lf[:, 0:1]
        bias.append(_dot_sel_l(rows_of_head, suf))
    carry_ref[...] = carry

    for i in range(G):
        for h in range(H_B):
            rows_h = pl.ds(h, P, stride=H_B)
            kbf_ref[i * P:(i + 1) * P, hsl(h)] = k_refs[i][0, rows_h, :].astype(BF16)
            vbf_ref[i * P:(i + 1) * P, hsl(h)] = v_refs[i][0, rows_h, :].astype(BF16)
    st = lax.dot_general(kbf_ref[...], qbd_ref[...], (((1,), (1,)), ((), ())), preferred_element_type=F32)
    st = st + cq_ref[...]
    s_tiles = [st[i * P:(i + 1) * P, :].T + bias[i] for i in range(G)]
    _accumulate(s_tiles, lambda h: vbf_ref[:, hsl(h)], m_ref, l_ref, acc_ref, Tn)

    @pl.when(g == ngroups - 1)
    def _():
        pad = (-Tn) % 16
        for src_ref, dst_ref in ((kn_ref, kbf_ref), (vn_ref, vbf_ref)):
            top = jnp.concatenate([src_ref[...], jnp.zeros((pad, D), F32)], axis=0)
            dst_ref[0:Tn + pad, :] = top.astype(BF16)
            dst_ref[Tn + pad:P, :] = jnp.zeros((P - Tn - pad, D), BF16)
        stn = lax.dot_general(kbf_ref[0:P, :], qbd_ref[...], (((1,), (1,)), ((), ())), preferred_element_type=F32)
        own_sum = jnp.concatenate([xn_ref[...], jnp.zeros((P - Tn, HT), F32)], axis=0)
        visible = iota((P, HT), 0) <= iota((P, HT), 1) % Tn
        stn = jnp.where(visible, stn + cq_ref[...] - own_sum, -jnp.inf)
        _accumulate([stn.T], lambda h: vbf_ref[0:P, hsl(h)], m_ref, l_ref, acc_ref, Tn)
        l = l_ref[...]
        for h in range(H_B):
            o_ref[:, hsl(h)] = acc_ref[h * Tn:(h + 1) * Tn, :] / l[h * Tn:(h + 1) * Tn, :]


def _fox_sample(page_table, q, k, v, logf, cache_k, cache_v, cache_logf_t, *, Bd, Tn, G, row_off):
    D = D_MODEL
    npages = page_table.shape[1]
    page = cache_logf_t.shape[2]
    ngroups = npages // G
    base = row_off // Tn
    tokmap = lambda b, g, pt: (base + b, 0)
    in_specs = [pl.BlockSpec((Tn, D), tokmap), pl.BlockSpec((Tn, logf.shape[1]), tokmap),
                pl.BlockSpec((Tn, D), tokmap), pl.BlockSpec((Tn, D), tokmap)]
    for _ in range(2):
        in_specs += [pl.BlockSpec((1, page * H_B, HD_B),
                                  lambda b, g, pt, i=i: (pt[b, npages - 1 - (g * G + i)], 0, 0)) for i in range(G)]
    in_specs += [pl.BlockSpec((1, H_B, page),
                              lambda b, g, pt, i=i: (pt[b, npages - 1 - (g * G + i)], 0, 0)) for i in range(G)]
    HT = H_B * Tn
    grid_spec = pltpu.PrefetchScalarGridSpec(
        num_scalar_prefetch=1,
        grid=(Bd, ngroups),
        in_specs=in_specs,
        out_specs=pl.BlockSpec((Tn, D), lambda b, g, pt: (b, 0)),
        scratch_shapes=[pltpu.VMEM((HT, D), BF16), pltpu.VMEM((G * page, D), BF16), pltpu.VMEM((G * page, D), BF16),
                        pltpu.VMEM((HT, 1), F32), pltpu.VMEM((HT, 1), F32), pltpu.VMEM((HT, HD_B), F32),
                        pltpu.VMEM((H_B, 1), F32), pltpu.VMEM((1, HT), F32), pltpu.VMEM((Tn, HT), F32)],
    )
    return pl.pallas_call(
        functools.partial(_fox_sample_kernel, G=G, ngroups=ngroups),
        grid_spec=grid_spec,
        out_shape=jax.ShapeDtypeStruct((Bd * Tn, D), F32),
        compiler_params=_params("parallel", "arbitrary"),
        name="fox_sample",
    )(page_table, q, logf, k, v, *([cache_k] * G), *([cache_v] * G), *([cache_logf_t] * G))


def _shift_rows(seq, first, n):
    T = seq.shape[1]
    return jnp.concatenate([first, seq], axis=1)[:, :T]


def kernel(x_prompt, x_sample, state_wkv, state_shift, state_conv, cache_k, cache_v, cache_logf, page_table, att_norm, time_mix, w_rkv, w0, w1, w2, a0, a1, a2, g1, g2, k_k, k_a, r_k, lnx_w, lnx_b, w_o_a, ffn_norm, w_ug, conv_w, conv_b, w_down, kv_norm, w_kvf, b_f, k_norm, attn_norm_b, w_q, q_norm, w_o_b):
    D = D_MODEL
    Bp, Tp, _ = x_prompt.shape
    Bd, Td, _ = x_sample.shape
    Mp, Md = Bp * Tp, Bd * Td
    F = conv_b.shape[1]
    bf = lambda t: t.astype(BF16)

    def split(t):
        return t[:Mp].reshape(Bp, Tp, -1), t[Mp:].reshape(Bd, Td, -1)

    def join(tp, ts):
        return jnp.concatenate([tp.reshape(Mp, -1), ts.reshape(Md, -1)], axis=0)

    x = join(x_prompt, x_sample)

    xn = _rmsnorm(x, att_norm[0], tm=384)
    xn_p, xn_s = split(xn)
    xprev = join(_shift_rows(xn_p, jnp.zeros((Bp, 1, D), F32), 1), _shift_rows(xn_s, state_shift[0][:, None], 1))
    mix = time_mix[0]
    rkv = _rkv(xn, xprev, mix[jnp.array([0, 2, 3])], bf(w_rkv[0]), tm=768, tn=1024)
    pad_c = lambda t: jnp.pad(t, ((0, 0), (0, LORA_PAD - t.shape[1])))
    pad_r = lambda t: jnp.pad(t, ((0, LORA_PAD - t.shape[0]), (0, 0)))
    lw, a, g = _lora(xn, xprev, mix[jnp.array([1, 4, 5])], bf(pad_c(w1[0])), bf(pad_c(a1[0])), bf(g1[0]),
                     bf(pad_r(w2[0])), bf(pad_r(a2[0])), bf(g2[0]), w0[0], a0[0], tm=384)
    wkv_args = (rkv, lw, a, g, k_k[0], k_a[0], r_k[0], lnx_w[0], lnx_b[0])
    z_p, p_wkv = _wkv(*wkv_args, jnp.zeros((Bp, H_A, HEAD_A, HEAD_A), F32), B=Bp, T=Tp, C=64, HB=8, row_off=0)
    z_s, s_wkv = _wkv(*wkv_args, state_wkv[0], B=Bd, T=Td, C=Td, HB=16, row_off=Mp)
    x = _mm(join(z_p, z_s), bf(w_o_a[0]), tm=768, tn=1024, epi="res", extra=x, name="wkv_out_proj")

    def conv_ffn(x, layer, conv0_s):
        ug = _mm(x, bf(w_ug[layer]), tm=768, tn=1024, gain=ffn_norm[layer], name="ffn_up")
        gt_p, gt_s = split(ug[:, F:])
        zeros2 = jnp.zeros((Bp, 2, F), F32)
        gm1 = join(_shift_rows(gt_p, zeros2[:, 1:], 1), _shift_rows(gt_s, conv0_s[:, 1:], 1))
        gm2 = join(_shift_rows(gt_p, zeros2, 2), _shift_rows(gt_s, conv0_s, 2))
        y = _ffn_down(ug, gm1, gm2, conv_w[layer], conv_b[layer], bf(w_down[layer]), x, tm=384, tk=1408)
        return y, gt_p[:, -2:], gt_s[:, -2:]

    x, p_conv0, s_conv0 = conv_ffn(x, 0, state_conv[0])

    k_all = _mm(x, bf(w_kvf[:, :D]), tm=768, tn=1024, gain=kv_norm, epi="headrms", extra=k_norm, name="k_proj")
    v_all = _mm(x, bf(w_kvf[:, D:2 * D]), tm=768, tn=1024, gain=kv_norm, name="v_proj")
    w_f = jnp.pad(w_kvf[:, 2 * D:], ((0, 0), (0, HD_B - H_B)))
    logf_all = _mm(x, bf(w_f), tm=768, tn=HD_B, gain=kv_norm, epi="logsig", extra=jnp.pad(b_f, (0, HD_B - H_B)),
                   name="logf_proj")

    q_all = _mm(x, bf(w_q[0]), tm=768, tn=1024, gain=attn_norm_b[0], epi="headrms", extra=q_norm[0],
                scale=ATTN_SCALE, name="q_proj")
    c_col = _cumsum_rows(logf_all, B=Bp, T=Tp, tc=512)
    c_row = jnp.transpose(c_col.reshape(Bp, Tp, HD_B)[:, :, :H_B], (0, 2, 1))
    o_p = _fox_prompt(q_all, k_all, v_all, c_col, c_row, B=Bp, T=Tp, tq=256, tk=512)

    pool_rows = lambda t: t.reshape(t.shape[0], t.shape[1] * H_B, HD_B)
    o_s = _fox_sample(page_table, q_all, k_all, v_all, logf_all, pool_rows(cache_k), pool_rows(cache_v),
                      jnp.transpose(cache_logf, (0, 2, 1)), Bd=Bd, Tn=Td, G=4, row_off=Mp)
    x = _mm(join(o_p, o_s), bf(w_o_b[0]), tm=768, tn=1024, epi="res", extra=x, name="attn_out_proj")

    x, p_conv1, s_conv1 = conv_ffn(x, 1, state_conv[1])

    y_p, y_s = split(x)
    k_p, k_s = split(k_all)
    v_p, v_s = split(v_all)
    lf_p, lf_s = split(logf_all[:, :H_B])
    heads = lambda t: t.reshape(t.shape[0], t.shape[1], H_B, HD_B)
    return (y_p, y_s,
            p_wkv[None], xn_p[:, -1][None], jnp.stack([p_conv0, p_conv1]), heads(k_p), heads(v_p), lf_p,
            s_wkv[None], xn_s[:, -1][None], jnp.stack([s_conv0, s_conv1]), heads(k_s), heads(v_s), lf_s)
```

```python
import functools
import math

import jax
import jax.numpy as jnp
from jax import lax
from jax.experimental import pallas as pl
from jax.experimental.pallas import tpu as pltpu

F32 = jnp.float32
BF16 = jnp.bfloat16

D_MODEL = 2048
HEAD_A = 64
H_A = D_MODEL // HEAD_A
HD_B = 128
H_B = D_MODEL // HD_B
NORM_EPS = 1e-6
GN_EPS = 64e-5
ATTN_SCALE = HD_B ** -0.5
LORA_PAD = 128
VMEM_LIMIT = 56 * 1024 * 1024


def _params(*sem):
    return pltpu.CompilerParams(dimension_semantics=sem, vmem_limit_bytes=VMEM_LIMIT)


def _sigmoid(x):
    return 1.0 / (1.0 + jnp.exp(-x))


def _log_sigmoid(z):
    return jnp.minimum(z, 0.0) - jnp.log1p(jnp.exp(-jnp.abs(z)))


def _dot(a, b):
    return jnp.dot(a.astype(BF16), b.astype(BF16), preferred_element_type=F32)


def _dot_nt(a, b):
    return lax.dot_general(a.astype(BF16), b.astype(BF16), (((1,), (1,)), ((), ())), preferred_element_type=F32)


def _dot_tn(a, b):
    return lax.dot_general(a.astype(BF16), b.astype(BF16), (((0,), (0,)), ((), ())), preferred_element_type=F32)


def _split2(x):
    hi = x.astype(BF16)
    return hi, (x - hi.astype(F32)).astype(BF16)


def _split3(x):
    hi = x.astype(BF16)
    r1 = x - hi.astype(F32)
    mid = r1.astype(BF16)
    return hi, mid, (r1 - mid.astype(F32)).astype(BF16)


def _dot3(a, b):
    ah, al = _split2(a)
    bh, bl = _split2(b)
    d = lambda x, y: jnp.dot(x, y, preferred_element_type=F32)
    return d(ah, bh) + (d(ah, bl) + d(al, bh))


def _dot_sel_l(sel, x):
    hi, mid, lo = _split3(x)
    d = lambda p: jnp.dot(sel, p, preferred_element_type=F32)
    return d(hi) + (d(mid) + d(lo))


def _dot_sel_r(x, sel):
    hi, mid, lo = _split3(x)
    d = lambda p: jnp.dot(p, sel, preferred_element_type=F32)
    return d(hi) + (d(mid) + d(lo))


HALO = 8


def _token_pos(shape, m, tm, seqs):
    Mp, Tp, Td = seqs
    r_in = lax.broadcasted_iota(jnp.int32, shape, 0)
    g = m * tm + r_in
    return r_in, jnp.where(g < Mp, g % Tp, (g - Mp) % Td)


def _delayed(x, halo, n, r_in):
    out = pltpu.roll(x, n, 0)
    for j in range(n):
        out = jnp.where(r_in == j, halo[HALO - n + j:HALO - n + j + 1, :], out)
    return out


def _rms_kernel(x_ref, halo_ref, st_ref, g_ref, o_ref, prev_ref, *, tm, nm, seqs):
    m = pl.program_id(0)

    def norm(x):
        ms = jnp.mean(x * x, axis=-1, keepdims=True)
        return x * lax.rsqrt(ms + NORM_EPS) * g_ref[...]

    xn = norm(x_ref[...])
    o_ref[...] = xn
    r_in, tpos = _token_pos(xn.shape, m, tm, seqs)
    first = jnp.where(m == nm - 1, st_ref[...], 0.0)
    prev_ref[...] = jnp.where(tpos >= 1, _delayed(xn, norm(halo_ref[...]), 1, r_in), first)


def _last_tile_rows(state, tm, Td):
    Bd, n, W = state.shape
    rows = jnp.pad(state, ((0, 0), (0, Td - n), (0, 0))).reshape(Bd * Td, W)
    return jnp.pad(rows, ((tm - Bd * Td, 0), (0, 0)))


def _rmsnorm_shift(x, g, shift_state, *, tm, seqs):
    M, D = x.shape
    nm = M // tm
    out = jax.ShapeDtypeStruct((M, D), F32)
    return pl.pallas_call(
        functools.partial(_rms_kernel, tm=tm, nm=nm, seqs=seqs),
        grid=(nm,),
        in_specs=[pl.BlockSpec((tm, D), lambda i: (i, 0)),
                  pl.BlockSpec((HALO, D), lambda i: (jnp.maximum(i * (tm // HALO) - 1, 0), 0)),
                  pl.BlockSpec((tm, D), lambda i: (0, 0)),
                  pl.BlockSpec((1, D), lambda i: (0, 0))],
        out_specs=[pl.BlockSpec((tm, D), lambda i: (i, 0))] * 2,
        out_shape=[out, out],
        compiler_params=_params("parallel"),
        name="rmsnorm_shift",
    )(x, x, _last_tile_rows(shift_state[:, None], tm, seqs[2]), g.reshape(1, D))


def _rkv_kernel(xn_ref, xp_ref, mix_ref, w_ref, o_ref, lhs_ref):
    @pl.when(pl.program_id(2) == 0)
    def _():
        xn = xn_ref[...]
        lhs_ref[...] = (xn + (xp_ref[...] - xn) * mix_ref[0]).astype(BF16)

    o_ref[0] = jnp.dot(lhs_ref[...], w_ref[0], preferred_element_type=F32)


def _rkv(xn, xprev, mix3, w3, tm, tn):
    M, D = xn.shape
    N = w3.shape[2]
    return pl.pallas_call(
        _rkv_kernel,
        grid=(M // tm, 3, N // tn),
        in_specs=[pl.BlockSpec((tm, D), lambda m, c, n: (m, 0)),
                  pl.BlockSpec((tm, D), lambda m, c, n: (m, 0)),
                  pl.BlockSpec((1, 1, D), lambda m, c, n: (c, 0, 0)),
                  pl.BlockSpec((1, D, tn), lambda m, c, n: (c, 0, n))],
        out_specs=pl.BlockSpec((1, tm, tn), lambda m, c, n: (c, m, n)),
        out_shape=jax.ShapeDtypeStruct((3, M, N), F32),
        scratch_shapes=[pltpu.VMEM((tm, D), BF16)],
        compiler_params=_params("parallel", "arbitrary", "arbitrary"),
        name="rkv_proj",
    )(xn, xprev, mix3.reshape(3, 1, D), w3)


def _lora_kernel(xn_ref, xp_ref, mix_ref, w1_ref, a1_ref, g1_ref, w2_ref, a2_ref, g2_ref, w0_ref, a0_ref,
                 lw_ref, a_ref, g_ref):
    xn = xn_ref[...]
    xx = xp_ref[...] - xn
    xw = xn + xx * mix_ref[0:1, :]
    xa = xn + xx * mix_ref[1:2, :]
    xg = xn + xx * mix_ref[2:3, :]
    hw = jnp.tanh(_dot(xw, w1_ref[...]))
    wv = w0_ref[...] + _dot(hw, w2_ref[...])
    w_raw = _log_sigmoid(wv) - 0.5
    lw_ref[...] = -jnp.exp(w_raw)
    ha = _dot(xa, a1_ref[...])
    a_ref[...] = _sigmoid(a0_ref[...] + _dot(ha, a2_ref[...]))
    hg = _sigmoid(_dot(xg, g1_ref[...]))
    g_ref[...] = _dot(hg, g2_ref[...])


def _lora(xn, xprev, mix3, w1, a1, g1, w2, a2, g2, w0, a0, tm):
    M, D = xn.shape
    row = lambda m: (m, 0)
    full = lambda m: (0, 0)
    specs = [pl.BlockSpec((tm, D), row), pl.BlockSpec((tm, D), row), pl.BlockSpec((3, D), full)]
    specs += [pl.BlockSpec(t.shape, full) for t in (w1, a1, g1, w2, a2, g2)]
    specs += [pl.BlockSpec((1, D), full), pl.BlockSpec((1, D), full)]
    out = jax.ShapeDtypeStruct((M, D), F32)
    return pl.pallas_call(
        _lora_kernel,
        grid=(M // tm,),
        in_specs=specs,
        out_specs=[pl.BlockSpec((tm, D), row)] * 3,
        out_shape=[out, out, out],
        compiler_params=_params("parallel"),
        name="lora_branches",
    )(xn, xprev, mix3, w1, a1, g1, w2, a2, g2, w0.reshape(1, D), a0.reshape(1, D))


def _wkv_kernel(r_ref, k_ref, v_ref, lw_ref, a_ref, g_ref, kk_ref, ka_ref, rk_ref, lnw_ref, lnb_ref, s0_ref,
                z_ref, sout_ref, S_ref, *, C, HB, nchunks):
    c = pl.program_id(2)

    @pl.when(c == 0)
    def _():
        S_ref[...] = s0_ref[0]

    rows = lax.broadcasted_iota(jnp.int32, (C, C), 0)
    cols = lax.broadcasted_iota(jnp.int32, (C, C), 1)
    incl = cols <= rows
    strict = cols < rows
    eye = (rows == cols).astype(F32)

    lw = lw_ref[...]
    cum = _dot_sel_l(incl.astype(BF16), lw)
    cum_last = cum[C - 1:C, :]
    e_pos = jnp.exp(cum)
    e_neg = jnp.exp(-cum)
    e_prev = jnp.exp(cum - lw)
    e_tail = jnp.exp(cum_last - cum)
    g_last = jnp.exp(cum_last)

    r = r_ref[0]
    k = k_ref[0]
    v = v_ref[0]
    a = a_ref[...]
    kkf = k * kk_ref[...]
    k2 = k * (1.0 + (a - 1.0) * ka_ref[...])

    heads = range(HB)
    sls = [slice(h * HEAD_A, (h + 1) * HEAD_A) for h in heads]
    cat = lambda x, y: jnp.concatenate([x, y], axis=0)

    kk_l, b_l = [], []
    for sl in sls:
        kk_h = kkf[:, sl]
        nrm = jnp.sqrt(jnp.sum(kk_h * kk_h, axis=-1, keepdims=True))
        kk_h = kk_h / jnp.maximum(nrm, 1e-12)
        kk_l.append(kk_h)
        b_l.append(kk_h * a[:, sl])
    QR = [cat(kk_l[h] * e_prev[:, sls[h]], r[:, sls[h]] * e_pos[:, sls[h]]) for h in heads]
    Kd = [k2[:, sl] * e_neg[:, sl] for sl in sls]
    Bd = [b_l[h] * e_neg[:, sls[h]] for h in heads]
    sk = [_dot_nt(QR[h], Kd[h]) for h in heads]
    sb = [_dot_nt(QR[h], Bd[h]) for h in heads]
    LU = [cat(jnp.where(strict, sk[h][:C], 0.0), jnp.where(incl, sk[h][C:], 0.0)) for h in heads]
    Ub = [jnp.where(incl, sb[h][C:], 0.0) for h in heads]

    Q = [-jnp.where(strict, sb[h][:C], 0.0) for h in heads]
    T = [eye + Q[h] for h in heads]
    levels = int(math.log2(C))
    if levels > 1:
        Q = [_dot3(Q[h], Q[h]) for h in heads]
    for lvl in range(1, levels):
        if lvl < levels - 1:
            R = [_dot3(cat(T[h], Q[h]), Q[h]) for h in heads]
            T = [T[h] + R[h][:C] for h in heads]
            Q = [R[h][C:] for h in heads]
        else:
            T = [T[h] + _dot3(T[h], Q[h]) for h in heads]

    S = [S_ref[h] for h in heads]
    SQ = [_dot_nt(QR[h], S[h]) for h in heads]
    LV = [_dot(LU[h], v[:, sls[h]]) for h in heads]
    X = [_dot3(T[h], SQ[h][:C] + LV[h][:C]) for h in heads]
    Y = [SQ[h][C:] + LV[h][C:] - _dot(Ub[h], X[h]) for h in heads]
    for h in heads:
        sl = sls[h]
        KBt = cat(k2[:, sl] * e_tail[:, sl], -(b_l[h] * e_tail[:, sl]))
        S_ref[h] = S[h] * g_last[:, sl] + _dot_tn(cat(v[:, sl], X[h]), KBt)

    for h in heads:
        sl = sls[h]
        mu = jnp.mean(Y[h], axis=-1, keepdims=True)
        yc = Y[h] - mu
        var = jnp.mean(yc * yc, axis=-1, keepdims=True)
        yn = yc * lax.rsqrt(var + GN_EPS) * lnw_ref[:, sl] + lnb_ref[:, sl]
        bonus = jnp.sum(r[:, sl] * k2[:, sl] * rk_ref[:, sl], axis=-1, keepdims=True) * v[:, sl]
        z_ref[:, sl] = (yn + bonus) * g_ref[:, sl]

    @pl.when(c == nchunks - 1)
    def _():
        sout_ref[0] = S_ref[...]


def _wkv(rkv, lw, a, g, k_k, k_a, r_k, lnx_w, lnx_b, state, *, B, T, C, HB, row_off):
    D = D_MODEL
    W = HB * HEAD_A
    nchunks = T // C
    base = row_off // C
    tok = lambda b, hg, c: (base + b * nchunks + c, hg)
    par = lambda b, hg, c: (0, hg)
    st = lambda b, hg, c: (b, hg, 0, 0)
    in_specs = [pl.BlockSpec((1, C, W), lambda b, hg, c, i=i: (i, base + b * nchunks + c, hg)) for i in range(3)]
    in_specs += [pl.BlockSpec((C, W), tok)] * 3
    in_specs += [pl.BlockSpec((1, W), par)] * 5
    in_specs += [pl.BlockSpec((1, HB, HEAD_A, HEAD_A), st)]
    z, s_out = pl.pallas_call(
        functools.partial(_wkv_kernel, C=C, HB=HB, nchunks=nchunks),
        grid=(B, H_A // HB, nchunks),
        in_specs=in_specs,
        out_specs=[pl.BlockSpec((C, W), lambda b, hg, c: (b * nchunks + c, hg)),
                   pl.BlockSpec((1, HB, HEAD_A, HEAD_A), st)],
        out_shape=[jax.ShapeDtypeStruct((B * T, D), F32),
                   jax.ShapeDtypeStruct((B, H_A, HEAD_A, HEAD_A), F32)],
        scratch_shapes=[pltpu.VMEM((HB, HEAD_A, HEAD_A), F32)],
        compiler_params=_params("parallel", "parallel", "arbitrary"),
        name=f"wkv7_chunk{C}",
    )(rkv, rkv, rkv, lw, a, g, k_k.reshape(1, D), k_a.reshape(1, D), r_k.reshape(1, D),
      lnx_w.reshape(1, D), lnx_b.reshape(1, D), state)
    return z, s_out


def _mm_kernel(*refs, pro, epi, scale):
    it = iter(refs)
    x_ref = next(it)
    g_ref = next(it) if pro == "rms" else None
    w_ref = next(it)
    e_ref = next(it) if epi is not None else None
    o_ref = next(it)
    lhs_ref = next(it)

    @pl.when(pl.program_id(1) == 0)
    def _():
        x = x_ref[...]
        if pro == "rms":
            ms = jnp.mean(x * x, axis=-1, keepdims=True)
            x = x * lax.rsqrt(ms + NORM_EPS) * g_ref[...]
        lhs_ref[...] = x.astype(BF16)

    acc = jnp.dot(lhs_ref[...], w_ref[...], preferred_element_type=F32)
    if epi == "res":
        o_ref[...] = acc + e_ref[...]
    elif epi == "logsig":
        o_ref[...] = _log_sigmoid(acc + e_ref[...])
    elif epi == "headrms":
        for h in range(acc.shape[1] // HD_B):
            sl = slice(h * HD_B, (h + 1) * HD_B)
            t = acc[:, sl]
            ms = jnp.mean(t * t, axis=-1, keepdims=True)
            o_ref[:, sl] = t * lax.rsqrt(ms + NORM_EPS) * e_ref[...] * scale
    else:
        o_ref[...] = acc


def _mm(x, w, *, tm, tn, gain=None, epi=None, extra=None, scale=1.0, name="mm"):
    M, K = x.shape
    N = w.shape[1]
    pro = "rms" if gain is not None else "cast"
    args = [x]
    in_specs = [pl.BlockSpec((tm, K), lambda m, n: (m, 0))]
    if gain is not None:
        args.append(gain.reshape(1, K))
        in_specs.append(pl.BlockSpec((1, K), lambda m, n: (0, 0)))
    args.append(w)
    in_specs.append(pl.BlockSpec((K, tn), lambda m, n: (0, n)))
    if epi == "res":
        args.append(extra)
        in_specs.append(pl.BlockSpec((tm, tn), lambda m, n: (m, n)))
    elif epi == "logsig":
        args.append(extra.reshape(1, N))
        in_specs.append(pl.BlockSpec((1, tn), lambda m, n: (0, n)))
    elif epi == "headrms":
        args.append(extra.reshape(1, HD_B))
        in_specs.append(pl.BlockSpec((1, HD_B), lambda m, n: (0, 0)))
    return pl.pallas_call(
        functools.partial(_mm_kernel, pro=pro, epi=epi, scale=scale),
        grid=(M // tm, N // tn),
        in_specs=in_specs,
        out_specs=pl.BlockSpec((tm, tn), lambda m, n: (m, n)),
        out_shape=jax.ShapeDtypeStruct((M, N), F32),
        scratch_shapes=[pltpu.VMEM((tm, K), BF16)],
        compiler_params=_params("parallel", "arbitrary"),
        name=name,
    )(*args)


def _ffn_down_kernel(u_ref, g0_ref, halo_ref, s1_ref, s2_ref, cw_ref, cb_ref, w_ref, r_ref, o_ref, *, tm, nm, seqs):
    m = pl.program_id(0)
    kstep = pl.program_id(1)

    @pl.when(kstep == 0)
    def _():
        o_ref[...] = r_ref[...]

    g0 = g0_ref[...]
    halo = halo_ref[...]
    r_in, tpos = _token_pos(g0.shape, m, tm, seqs)
    is_last = m == nm - 1
    g1 = jnp.where(tpos >= 1, _delayed(g0, halo, 1, r_in), jnp.where(is_last, s1_ref[...], 0.0))
    g2 = jnp.where(tpos >= 2, _delayed(g0, halo, 2, r_in), jnp.where(is_last, s2_ref[...], 0.0))
    gc = cb_ref[...] + g2 * cw_ref[0:1, :]
    gc = gc + g1 * cw_ref[1:2, :]
    gc = gc + g0 * cw_ref[2:3, :]
    hid = gc * _sigmoid(gc) * u_ref[...]
    o_ref[...] += jnp.dot(hid.astype(BF16), w_ref[...], preferred_element_type=F32)


def _ffn_down(ug, conv_state, conv_w, conv_b, w_down, res, *, tm, tk, seqs):
    M = ug.shape[0]
    F, D = w_down.shape
    nk = F // tk
    nm = M // tm
    last_k = lambda m, k: (0, jnp.where(m == nm - 1, k, 0))
    return pl.pallas_call(
        functools.partial(_ffn_down_kernel, tm=tm, nm=nm, seqs=seqs),
        grid=(nm, nk),
        in_specs=[pl.BlockSpec((tm, tk), lambda m, k: (m, k)),
                  pl.BlockSpec((tm, tk), lambda m, k: (m, nk + k)),
                  pl.BlockSpec((HALO, tk), lambda m, k: (jnp.maximum(m * (tm // HALO) - 1, 0), nk + k)),
                  pl.BlockSpec((tm, tk), last_k),
                  pl.BlockSpec((tm, tk), last_k),
                  pl.BlockSpec((3, tk), lambda m, k: (0, k)),
                  pl.BlockSpec((1, tk), lambda m, k: (0, k)),
                  pl.BlockSpec((tk, D), lambda m, k: (k, 0)),
                  pl.BlockSpec((tm, D), lambda m, k: (m, 0))],
        out_specs=pl.BlockSpec((tm, D), lambda m, k: (m, 0)),
        out_shape=jax.ShapeDtypeStruct((M, D), F32),
        compiler_params=_params("parallel", "arbitrary"),
        name="ffn_down",
    )(ug, ug, ug, _last_tile_rows(conv_state[:, 1:], tm, seqs[2]), _last_tile_rows(conv_state, tm, seqs[2]),
      conv_w, conv_b.reshape(1, F), w_down, res)


def _cumsum_kernel(x_ref, o_ref, carry_ref):
    @pl.when(pl.program_id(1) == 0)
    def _():
        carry_ref[...] = jnp.zeros_like(carry_ref)

    tc = x_ref.shape[0]
    rows = lax.broadcasted_iota(jnp.int32, (tc, tc), 0)
    cols = lax.broadcasted_iota(jnp.int32, (tc, tc), 1)
    cs = _dot_sel_l((cols <= rows).astype(BF16), x_ref[...]) + carry_ref[...]
    o_ref[...] = cs
    carry_ref[...] = cs[tc - 1:tc, :]


def _cumsum_rows(x, *, B, T, tc):
    n = T // tc
    return pl.pallas_call(
        _cumsum_kernel,
        grid=(B, n),
        in_specs=[pl.BlockSpec((tc, x.shape[1]), lambda b, j: (b * n + j, 0))],
        out_specs=pl.BlockSpec((tc, x.shape[1]), lambda b, j: (b * n + j, 0)),
        out_shape=jax.ShapeDtypeStruct((B * T, x.shape[1]), F32),
        scratch_shapes=[pltpu.VMEM((1, x.shape[1]), F32)],
        compiler_params=_params("parallel", "arbitrary"),
        name="logf_cumsum",
    )(x)


def _online_update(s, v_bf, m_ref, l_ref, acc_ref, h, sl):
    m_prev = m_ref[h]
    m_new = jnp.maximum(m_prev, jnp.max(s, axis=-1, keepdims=True))
    alpha = jnp.exp(m_prev - m_new)
    p = jnp.exp(s - m_new)
    l_ref[h] = alpha * l_ref[h] + jnp.sum(p, axis=-1, keepdims=True)
    acc_ref[:, sl] = alpha * acc_ref[:, sl] + jnp.dot(p.astype(BF16), v_bf, preferred_element_type=F32)
    m_ref[h] = m_new


def _fox_prompt_kernel(q_ref, k_ref, v_ref, cq_ref, ck_ref, o_ref, m_ref, l_ref, acc_ref, *, tq, tk):
    qi = pl.program_id(1)
    kj = pl.program_id(2)
    last = (qi * tq + tq - 1) // tk

    @pl.when(kj == 0)
    def _():
        m_ref[...] = jnp.full_like(m_ref, -jnp.inf)
        l_ref[...] = jnp.zeros_like(l_ref)
        acc_ref[...] = jnp.zeros_like(acc_ref)

    def block(masked):
        if masked:
            rows = qi * tq + lax.broadcasted_iota(jnp.int32, (tq, tk), 0)
            cols = kj * tk + lax.broadcasted_iota(jnp.int32, (tq, tk), 1)
            visible = cols <= rows
        for h in range(H_B):
            sl = slice(h * HD_B, (h + 1) * HD_B)
            s = _dot_nt(q_ref[:, sl], k_ref[:, sl]) + cq_ref[:, h:h + 1] - ck_ref[0, h:h + 1, :]
            if masked:
                s = jnp.where(visible, s, -jnp.inf)
            _online_update(s, v_ref[:, sl].astype(BF16), m_ref, l_ref, acc_ref, h, sl)

    @pl.when(kj < last)
    def _():
        block(False)

    @pl.when(kj == last)
    def _():
        block(True)
        for h in range(H_B):
            sl = slice(h * HD_B, (h + 1) * HD_B)
            o_ref[:, sl] = acc_ref[:, sl] / l_ref[h]


def _fox_prompt(q, k, v, c_col, c_row, *, B, T, tq, tk):
    D = D_MODEL
    nq, nk = T // tq, T // tk
    qmap = lambda b, qi, kj: (b * nq + qi, 0)
    kblk = lambda qi, kj: jnp.minimum(kj, (qi * tq + tq - 1) // tk)
    kmap = lambda b, qi, kj: (b * nk + kblk(qi, kj), 0)
    return pl.pallas_call(
        functools.partial(_fox_prompt_kernel, tq=tq, tk=tk),
        grid=(B, nq, nk),
        in_specs=[pl.BlockSpec((tq, D), qmap), pl.BlockSpec((tk, D), kmap), pl.BlockSpec((tk, D), kmap),
                  pl.BlockSpec((tq, c_col.shape[1]), qmap),
                  pl.BlockSpec((1, H_B, tk), lambda b, qi, kj: (b, 0, kblk(qi, kj)))],
        out_specs=pl.BlockSpec((tq, D), qmap),
        out_shape=jax.ShapeDtypeStruct((B * T, D), F32),
        scratch_shapes=[pltpu.VMEM((H_B, tq, 1), F32), pltpu.VMEM((H_B, tq, 1), F32), pltpu.VMEM((tq, D), F32)],
        compiler_params=_params("parallel", "parallel", "arbitrary"),
        name="fox_prompt",
    )(q, k, v, c_col, c_row)


def _accumulate(s_tiles, v_of_head, m_ref, l_ref, acc_ref, Tn):
    m_prev = m_ref[...]
    m_new = jnp.maximum(m_prev, jnp.max(functools.reduce(jnp.maximum, s_tiles), axis=1, keepdims=True))
    alpha = jnp.exp(m_prev - m_new)
    p = [jnp.exp(t - m_new) for t in s_tiles]
    l_ref[...] = alpha * l_ref[...] + jnp.sum(functools.reduce(jnp.add, p), axis=1, keepdims=True)
    m_ref[...] = m_new
    pv = []
    for h in range(H_B):
        lhs = jnp.concatenate([t[h * Tn:(h + 1) * Tn, :] for t in p], axis=1).astype(BF16)
        pv.append(jnp.dot(lhs, v_of_head(h), preferred_element_type=F32))
    acc_ref[...] = alpha * acc_ref[...] + jnp.concatenate(pv, axis=0)


def _fox_sample_kernel(pt_ref, q_ref, lfn_ref, kn_ref, vn_ref, *rest, G, ngroups):
    k_refs, v_refs, lf_refs = rest[:G], rest[G:2 * G], rest[2 * G:3 * G]
    o_ref, qbd_ref, kbf_ref, vbf_ref, m_ref, l_ref, acc_ref, carry_ref, cq_ref, xn_ref = rest[3 * G:]
    g = pl.program_id(1)
    Tn, D = q_ref.shape
    P = k_refs[0].shape[1] // H_B
    HT = H_B * Tn
    iota = lambda shape, d: lax.broadcasted_iota(jnp.int32, shape, d)
    hsl = lambda h: slice(h * HD_B, (h + 1) * HD_B)

    @pl.when(g == 0)
    def _():
        q = q_ref[...]
        lane_head = iota((Tn, D), 1) // HD_B
        for hp in range(H_B // 2):
            pair = jnp.concatenate([jnp.where(lane_head == 2 * hp, q, 0.0),
                                    jnp.where(lane_head == 2 * hp + 1, q, 0.0)], axis=0)
            qbd_ref[hp * 2 * Tn:(hp + 1) * 2 * Tn, :] = pair.astype(BF16)
        x = lfn_ref[...]
        t_idx = iota(x.shape, 0)
        cs = jnp.zeros_like(x)
        for u in range(Tn):
            cs = cs + jnp.where(t_idx >= u, x[u:u + 1, :], 0.0)
        spread = (iota((x.shape[1], HT), 0) == iota((x.shape[1], HT), 1) // Tn).astype(BF16)
        xn = _dot_sel_r(cs, spread)
        xn_ref[...] = xn
        own = iota((Tn, HT), 0) == iota((Tn, HT), 1) % Tn
        cq_ref[...] = jnp.sum(jnp.where(own, xn, 0.0), axis=0, keepdims=True)
        m_ref[...] = jnp.full_like(m_ref, -jnp.inf)
        l_ref[...] = jnp.zeros_like(l_ref)
        acc_ref[...] = jnp.zeros_like(acc_ref)
        carry_ref[...] = jnp.zeros_like(carry_ref)

    after = (iota((P, P), 0) > iota((P, P), 1)).astype(BF16)
    rows_of_head = (iota((HT, H_B), 0) // Tn == iota((HT, H_B), 1)).astype(BF16)
    carry = carry_ref[...]
    bias = []
    for i in range(G):
        lf = lf_refs[i][0]
        suf = _dot_sel_r(lf, after) + carry
        carry = suf[:, 0:1] + lf[:, 0:1]
        bias.append(_dot_sel_l(rows_of_head, suf))
    carry_ref[...] = carry

    for i in range(G):
        for h in range(H_B):
            rows_h = pl.ds(h, P, stride=H_B)
            kbf_ref[i * P:(i + 1) * P, hsl(h)] = k_refs[i][0, rows_h, :].astype(BF16)
            vbf_ref[i * P:(i + 1) * P, hsl(h)] = v_refs[i][0, rows_h, :].astype(BF16)
    st = lax.dot_general(kbf_ref[...], qbd_ref[...], (((1,), (1,)), ((), ())), preferred_element_type=F32)
    st = st + cq_ref[...]
    s_tiles = [st[i * P:(i + 1) * P, :].T + bias[i] for i in range(G)]
    _accumulate(s_tiles, lambda h: vbf_ref[:, hsl(h)], m_ref, l_ref, acc_ref, Tn)

    @pl.when(g == ngroups - 1)
    def _():
        pad = (-Tn) % 16
        for src_ref, dst_ref in ((kn_ref, kbf_ref), (vn_ref, vbf_ref)):
            top = jnp.concatenate([src_ref[...], jnp.zeros((pad, D), F32)], axis=0)
            dst_ref[0:Tn + pad, :] = top.astype(BF16)
            dst_ref[Tn + pad:P, :] = jnp.zeros((P - Tn - pad, D), BF16)
        stn = lax.dot_general(kbf_ref[0:P, :], qbd_ref[...], (((1,), (1,)), ((), ())), preferred_element_type=F32)
        own_sum = jnp.concatenate([xn_ref[...], jnp.zeros((P - Tn, HT), F32)], axis=0)
        visible = iota((P, HT), 0) <= iota((P, HT), 1) % Tn
        stn = jnp.where(visible, stn + cq_ref[...] - own_sum, -jnp.inf)
        _accumulate([stn.T], lambda h: vbf_ref[0:P, hsl(h)], m_ref, l_ref, acc_ref, Tn)
        l = l_ref[...]
        for h in range(H_B):
            o_ref[:, hsl(h)] = acc_ref[h * Tn:(h + 1) * Tn, :] / l[h * Tn:(h + 1) * Tn, :]


def _fox_sample(page_table, q, k, v, logf, cache_k, cache_v, cache_logf_t, *, Bd, Tn, G, row_off):
    D = D_MODEL
    npages = page_table.shape[1]
    page = cache_logf_t.shape[2]
    ngroups = npages // G
    base = row_off // Tn
    tokmap = lambda b, g, pt: (base + b, 0)
    in_specs = [pl.BlockSpec((Tn, D), tokmap), pl.BlockSpec((Tn, logf.shape[1]), tokmap),
                pl.BlockSpec((Tn, D), tokmap), pl.BlockSpec((Tn, D), tokmap)]
    for _ in range(2):
        in_specs += [pl.BlockSpec((1, page * H_B, HD_B),
                                  lambda b, g, pt, i=i: (pt[b, npages - 1 - (g * G + i)], 0, 0)) for i in range(G)]
    in_specs += [pl.BlockSpec((1, H_B, page),
                              lambda b, g, pt, i=i: (pt[b, npages - 1 - (g * G + i)], 0, 0)) for i in range(G)]
    HT = H_B * Tn
    grid_spec = pltpu.PrefetchScalarGridSpec(
        num_scalar_prefetch=1,
        grid=(Bd, ngroups),
        in_specs=in_specs,
        out_specs=pl.BlockSpec((Tn, D), lambda b, g, pt: (b, 0)),
        scratch_shapes=[pltpu.VMEM((HT, D), BF16), pltpu.VMEM((G * page, D), BF16), pltpu.VMEM((G * page, D), BF16),
                        pltpu.VMEM((HT, 1), F32), pltpu.VMEM((HT, 1), F32), pltpu.VMEM((HT, HD_B), F32),
                        pltpu.VMEM((H_B, 1), F32), pltpu.VMEM((1, HT), F32), pltpu.VMEM((Tn, HT), F32)],
    )
    return pl.pallas_call(
        functools.partial(_fox_sample_kernel, G=G, ngroups=ngroups),
        grid_spec=grid_spec,
        out_shape=jax.ShapeDtypeStruct((Bd * Tn, D), F32),
        compiler_params=_params("parallel", "arbitrary"),
        name="fox_sample",
    )(page_table, q, logf, k, v, *([cache_k] * G), *([cache_v] * G), *([cache_logf_t] * G))


def kernel(x_prompt, x_sample, state_wkv, state_shift, state_conv, cache_k, cache_v, cache_logf, page_table, att_norm, time_mix, w_rkv, w0, w1, w2, a0, a1, a2, g1, g2, k_k, k_a, r_k, lnx_w, lnx_b, w_o_a, ffn_norm, w_ug, conv_w, conv_b, w_down, kv_norm, w_kvf, b_f, k_norm, attn_norm_b, w_q, q_norm, w_o_b):
    D = D_MODEL
    Bp, Tp, _ = x_prompt.shape
    Bd, Td, _ = x_sample.shape
    Mp, Md = Bp * Tp, Bd * Td
    F = conv_b.shape[1]
    bf = lambda t: t.astype(BF16)

    def split(t):
        return t[:Mp].reshape(Bp, Tp, -1), t[Mp:].reshape(Bd, Td, -1)

    def join(tp, ts):
        return jnp.concatenate([tp.reshape(Mp, -1), ts.reshape(Md, -1)], axis=0)

    x = join(x_prompt, x_sample)

    seqs = (Mp, Tp, Td)

    def last_rows(t, n, c0=0):
        prompt = jnp.stack([t[(b + 1) * Tp - n:(b + 1) * Tp, c0:] for b in range(Bp)])
        return prompt, t[Mp:, c0:].reshape(Bd, Td, -1)[:, Td - n:]

    xn, xprev = _rmsnorm_shift(x, att_norm[0], state_shift[0], tm=384, seqs=seqs)
    p_shift, s_shift = last_rows(xn, 1)
    mix = time_mix[0]
    rkv = _rkv(xn, xprev, mix[jnp.array([0, 2, 3])], bf(w_rkv[0]), tm=768, tn=1024)
    pad_c = lambda t: jnp.pad(t, ((0, 0), (0, LORA_PAD - t.shape[1])))
    pad_r = lambda t: jnp.pad(t, ((0, LORA_PAD - t.shape[0]), (0, 0)))
    lw, a, g = _lora(xn, xprev, mix[jnp.array([1, 4, 5])], bf(pad_c(w1[0])), bf(pad_c(a1[0])), bf(g1[0]),
                     bf(pad_r(w2[0])), bf(pad_r(a2[0])), bf(g2[0]), w0[0], a0[0], tm=384)
    wkv_args = (rkv, lw, a, g, k_k[0], k_a[0], r_k[0], lnx_w[0], lnx_b[0])
    z_p, p_wkv = _wkv(*wkv_args, jnp.zeros((Bp, H_A, HEAD_A, HEAD_A), F32), B=Bp, T=Tp, C=64, HB=16, row_off=0)
    z_s, s_wkv = _wkv(*wkv_args, state_wkv[0], B=Bd, T=Td, C=Td, HB=16, row_off=Mp)
    x = _mm(join(z_p, z_s), bf(w_o_a[0]), tm=768, tn=1024, epi="res", extra=x, name="wkv_out_proj")

    def conv_ffn(x, layer, conv0_s):
        ug = _mm(x, bf(w_ug[layer]), tm=768, tn=1024, gain=ffn_norm[layer], name="ffn_up")
        y = _ffn_down(ug, conv0_s, conv_w[layer], conv_b[layer], bf(w_down[layer]), x, tm=384, tk=1408, seqs=seqs)
        return (y,) + last_rows(ug, 2, F)

    x, p_conv0, s_conv0 = conv_ffn(x, 0, state_conv[0])

    k_all = _mm(x, bf(w_kvf[:, :D]), tm=768, tn=1024, gain=kv_norm, epi="headrms", extra=k_norm, name="k_proj")
    v_all = _mm(x, bf(w_kvf[:, D:2 * D]), tm=768, tn=1024, gain=kv_norm, name="v_proj")
    w_f = jnp.pad(w_kvf[:, 2 * D:], ((0, 0), (0, HD_B - H_B)))
    logf_all = _mm(x, bf(w_f), tm=768, tn=HD_B, gain=kv_norm, epi="logsig", extra=jnp.pad(b_f, (0, HD_B - H_B)),
                   name="logf_proj")

    q_all = _mm(x, bf(w_q[0]), tm=768, tn=1024, gain=attn_norm_b[0], epi="headrms", extra=q_norm[0],
                scale=ATTN_SCALE, name="q_proj")
    c_col = _cumsum_rows(logf_all, B=Bp, T=Tp, tc=512)
    c_row = jnp.transpose(c_col.reshape(Bp, Tp, HD_B)[:, :, :H_B], (0, 2, 1))
    o_p = _fox_prompt(q_all, k_all, v_all, c_col, c_row, B=Bp, T=Tp, tq=256, tk=512)

    pool_rows = lambda t: t.reshape(t.shape[0], t.shape[1] * H_B, HD_B)
    o_s = _fox_sample(page_table, q_all, k_all, v_all, logf_all, pool_rows(cache_k), pool_rows(cache_v),
                      jnp.transpose(cache_logf, (0, 2, 1)), Bd=Bd, Tn=Td, G=8, row_off=Mp)
    x = _mm(join(o_p, o_s), bf(w_o_b[0]), tm=768, tn=1024, epi="res", extra=x, name="attn_out_proj")

    x, p_conv1, s_conv1 = conv_ffn(x, 1, state_conv[1])

    y_p, y_s = split(x)
    k_p, k_s = split(k_all)
    v_p, v_s = split(v_all)
    lf_p, lf_s = split(logf_all[:, :H_B])
    heads = lambda t: t.reshape(t.shape[0], t.shape[1], H_B, HD_B)
    return (y_p, y_s,
            p_wkv[None], p_shift[:, 0][None], jnp.stack([p_conv0, p_conv1]), heads(k_p), heads(v_p), lf_p,
            s_wkv[None], s_shift[:, 0][None], jnp.stack([s_conv0, s_conv1]), heads(k_s), heads(v_s), lf_s)
```

```python
import functools
import math

import jax
import jax.numpy as jnp
from jax import lax
from jax.experimental import pallas as pl
from jax.experimental.pallas import tpu as pltpu

F32 = jnp.float32
BF16 = jnp.bfloat16

D_MODEL = 2048
HEAD_A = 64
H_A = D_MODEL // HEAD_A
HD_B = 128
H_B = D_MODEL // HD_B
NORM_EPS = 1e-6
GN_EPS = 64e-5
ATTN_SCALE = HD_B ** -0.5
LORA_PAD = 128
VMEM_LIMIT = 56 * 1024 * 1024


def _params(*sem):
    return pltpu.CompilerParams(dimension_semantics=sem, vmem_limit_bytes=VMEM_LIMIT)


def _sigmoid(x):
    return 1.0 / (1.0 + jnp.exp(-x))


def _log_sigmoid(z):
    return jnp.minimum(z, 0.0) - jnp.log1p(jnp.exp(-jnp.abs(z)))


def _dot(a, b):
    return jnp.dot(a.astype(BF16), b.astype(BF16), preferred_element_type=F32)


def _dot_nt(a, b):
    return lax.dot_general(a.astype(BF16), b.astype(BF16), (((1,), (1,)), ((), ())), preferred_element_type=F32)


def _dot_tn(a, b):
    return lax.dot_general(a.astype(BF16), b.astype(BF16), (((0,), (0,)), ((), ())), preferred_element_type=F32)


def _split2(x):
    hi = x.astype(BF16)
    return hi, (x - hi.astype(F32)).astype(BF16)


def _split3(x):
    hi = x.astype(BF16)
    r1 = x - hi.astype(F32)
    mid = r1.astype(BF16)
    return hi, mid, (r1 - mid.astype(F32)).astype(BF16)


def _dot3(a, b):
    ah, al = _split2(a)
    bh, bl = _split2(b)
    d = lambda x, y: jnp.dot(x, y, preferred_element_type=F32)
    return d(ah, bh) + (d(ah, bl) + d(al, bh))


def _dot_sel_l(sel, x):
    hi, mid, lo = _split3(x)
    d = lambda p: jnp.dot(sel, p, preferred_element_type=F32)
    return d(hi) + (d(mid) + d(lo))


def _dot_sel_r(x, sel):
    hi, mid, lo = _split3(x)
    d = lambda p: jnp.dot(p, sel, preferred_element_type=F32)
    return d(hi) + (d(mid) + d(lo))


HALO = 8


def _token_pos(shape, m, tm, seqs):
    Mp, Tp, Td = seqs
    r_in = lax.broadcasted_iota(jnp.int32, shape, 0)
    g = m * tm + r_in
    return r_in, jnp.where(g < Mp, g % Tp, (g - Mp) % Td)


def _delayed(x, halo, n, r_in):
    out = pltpu.roll(x, n, 0)
    for j in range(n):
        out = jnp.where(r_in == j, halo[HALO - n + j:HALO - n + j + 1, :], out)
    return out


def _rms_kernel(x_ref, halo_ref, st_ref, g_ref, o_ref, prev_ref, *, tm, nm, seqs):
    m = pl.program_id(0)

    def norm(x):
        ms = jnp.mean(x * x, axis=-1, keepdims=True)
        return x * lax.rsqrt(ms + NORM_EPS) * g_ref[...]

    xn = norm(x_ref[...])
    o_ref[...] = xn
    r_in, tpos = _token_pos(xn.shape, m, tm, seqs)
    first = jnp.where(m == nm - 1, st_ref[...], 0.0)
    prev_ref[...] = jnp.where(tpos >= 1, _delayed(xn, norm(halo_ref[...]), 1, r_in), first)


def _last_tile_rows(state, tm, Td):
    Bd, n, W = state.shape
    rows = jnp.pad(state, ((0, 0), (0, Td - n), (0, 0))).reshape(Bd * Td, W)
    return jnp.pad(rows, ((tm - Bd * Td, 0), (0, 0)))


def _rmsnorm_shift(x, g, shift_state, *, tm, seqs):
    M, D = x.shape
    nm = M // tm
    out = jax.ShapeDtypeStruct((M, D), F32)
    return pl.pallas_call(
        functools.partial(_rms_kernel, tm=tm, nm=nm, seqs=seqs),
        grid=(nm,),
        in_specs=[pl.BlockSpec((tm, D), lambda i: (i, 0)),
                  pl.BlockSpec((HALO, D), lambda i: (jnp.maximum(i * (tm // HALO) - 1, 0), 0)),
                  pl.BlockSpec((tm, D), lambda i: (0, 0)),
                  pl.BlockSpec((1, D), lambda i: (0, 0))],
        out_specs=[pl.BlockSpec((tm, D), lambda i: (i, 0))] * 2,
        out_shape=[out, out],
        compiler_params=_params("parallel"),
        name="rmsnorm_shift",
    )(x, x, _last_tile_rows(shift_state[:, None], tm, seqs[2]), g.reshape(1, D))


def _rkv_kernel(xn_ref, xp_ref, mix_ref, w_ref, o_ref, lhs_ref):
    @pl.when(pl.program_id(2) == 0)
    def _():
        xn = xn_ref[...]
        lhs_ref[...] = (xn + (xp_ref[...] - xn) * mix_ref[0]).astype(BF16)

    o_ref[0] = jnp.dot(lhs_ref[...], w_ref[0].astype(BF16), preferred_element_type=F32)


def _rkv(xn, xprev, mix3, w3, tm, tn):
    M, D = xn.shape
    N = w3.shape[2]
    return pl.pallas_call(
        _rkv_kernel,
        grid=(M // tm, 3, N // tn),
        in_specs=[pl.BlockSpec((tm, D), lambda m, c, n: (m, 0)),
                  pl.BlockSpec((tm, D), lambda m, c, n: (m, 0)),
                  pl.BlockSpec((1, 1, D), lambda m, c, n: (c, 0, 0)),
                  pl.BlockSpec((1, D, tn), lambda m, c, n: (c, 0, n))],
        out_specs=pl.BlockSpec((1, tm, tn), lambda m, c, n: (c, m, n)),
        out_shape=jax.ShapeDtypeStruct((3, M, N), F32),
        scratch_shapes=[pltpu.VMEM((tm, D), BF16)],
        compiler_params=_params("parallel", "arbitrary", "arbitrary"),
        name="rkv_proj",
    )(xn, xprev, mix3.reshape(3, 1, D), w3)


def _lora_kernel(xn_ref, xp_ref, mix_ref, w1_ref, a1_ref, g1_ref, w2_ref, a2_ref, g2_ref, w0_ref, a0_ref,
                 lw_ref, a_ref, g_ref):
    xn = xn_ref[...]
    xx = xp_ref[...] - xn
    xw = xn + xx * mix_ref[0:1, :]
    xa = xn + xx * mix_ref[1:2, :]
    xg = xn + xx * mix_ref[2:3, :]
    hw = jnp.tanh(_dot(xw, w1_ref[...]))
    wv = w0_ref[...] + _dot(hw, w2_ref[...])
    w_raw = _log_sigmoid(wv) - 0.5
    lw_ref[...] = -jnp.exp(w_raw)
    ha = _dot(xa, a1_ref[...])
    a_ref[...] = _sigmoid(a0_ref[...] + _dot(ha, a2_ref[...]))
    hg = _sigmoid(_dot(xg, g1_ref[...]))
    g_ref[...] = _dot(hg, g2_ref[...])


def _lora(xn, xprev, mix3, w1, a1, g1, w2, a2, g2, w0, a0, tm):
    M, D = xn.shape
    row = lambda m: (m, 0)
    full = lambda m: (0, 0)
    specs = [pl.BlockSpec((tm, D), row), pl.BlockSpec((tm, D), row), pl.BlockSpec((3, D), full)]
    specs += [pl.BlockSpec(t.shape, full) for t in (w1, a1, g1, w2, a2, g2)]
    specs += [pl.BlockSpec((1, D), full), pl.BlockSpec((1, D), full)]
    out = jax.ShapeDtypeStruct((M, D), F32)
    return pl.pallas_call(
        _lora_kernel,
        grid=(M // tm,),
        in_specs=specs,
        out_specs=[pl.BlockSpec((tm, D), row)] * 3,
        out_shape=[out, out, out],
        compiler_params=_params("parallel"),
        name="lora_branches",
    )(xn, xprev, mix3, w1, a1, g1, w2, a2, g2, w0.reshape(1, D), a0.reshape(1, D))


def _wkv_kernel(r_ref, k_ref, v_ref, lw_ref, a_ref, g_ref, kk_ref, ka_ref, rk_ref, lnw_ref, lnb_ref, s0_ref,
                z_ref, sout_ref, S_ref, *, C, HB, nchunks):
    c = pl.program_id(2)

    @pl.when(c == 0)
    def _():
        S_ref[...] = s0_ref[0]

    rows = lax.broadcasted_iota(jnp.int32, (C, C), 0)
    cols = lax.broadcasted_iota(jnp.int32, (C, C), 1)
    incl = cols <= rows
    strict = cols < rows
    eye = (rows == cols).astype(F32)

    lw = lw_ref[...]
    cum = _dot_sel_l(incl.astype(BF16), lw)
    cum_last = cum[C - 1:C, :]
    e_pos = jnp.exp(cum)
    e_neg = jnp.exp(-cum)
    e_prev = jnp.exp(cum - lw)
    e_tail = jnp.exp(cum_last - cum)
    g_last = jnp.exp(cum_last)

    r = r_ref[0]
    k = k_ref[0]
    v = v_ref[0]
    a = a_ref[...]
    kkf = k * kk_ref[...]
    k2 = k * (1.0 + (a - 1.0) * ka_ref[...])

    heads = range(HB)
    sls = [slice(h * HEAD_A, (h + 1) * HEAD_A) for h in heads]
    cat = lambda x, y: jnp.concatenate([x, y], axis=0)

    kk_l, b_l = [], []
    for sl in sls:
        kk_h = kkf[:, sl]
        nrm = jnp.sqrt(jnp.sum(kk_h * kk_h, axis=-1, keepdims=True))
        kk_h = kk_h / jnp.maximum(nrm, 1e-12)
        kk_l.append(kk_h)
        b_l.append(kk_h * a[:, sl])
    QR = [cat(kk_l[h] * e_prev[:, sls[h]], r[:, sls[h]] * e_pos[:, sls[h]]) for h in heads]
    Kd = [k2[:, sl] * e_neg[:, sl] for sl in sls]
    Bd = [b_l[h] * e_neg[:, sls[h]] for h in heads]
    sk = [_dot_nt(QR[h], Kd[h]) for h in heads]
    sb = [_dot_nt(QR[h], Bd[h]) for h in heads]
    LU = [cat(jnp.where(strict, sk[h][:C], 0.0), jnp.where(incl, sk[h][C:], 0.0)) for h in heads]
    Ub = [jnp.where(incl, sb[h][C:], 0.0) for h in heads]

    Q = [-jnp.where(strict, sb[h][:C], 0.0) for h in heads]
    T = [eye + Q[h] for h in heads]
    levels = int(math.log2(C))
    if levels > 1:
        Q = [_dot3(Q[h], Q[h]) for h in heads]
    for lvl in range(1, levels):
        if lvl < levels - 1:
            R = [_dot3(cat(T[h], Q[h]), Q[h]) for h in heads]
            T = [T[h] + R[h][:C] for h in heads]
            Q = [R[h][C:] for h in heads]
        else:
            T = [T[h] + _dot3(T[h], Q[h]) for h in heads]

    S = [S_ref[h] for h in heads]
    SQ = [_dot_nt(QR[h], S[h]) for h in heads]
    LV = [_dot(LU[h], v[:, sls[h]]) for h in heads]
    X = [_dot3(T[h], SQ[h][:C] + LV[h][:C]) for h in heads]
    Y = [SQ[h][C:] + LV[h][C:] - _dot(Ub[h], X[h]) for h in heads]
    for h in heads:
        sl = sls[h]
        KBt = cat(k2[:, sl] * e_tail[:, sl], -(b_l[h] * e_tail[:, sl]))
        S_ref[h] = S[h] * g_last[:, sl] + _dot_tn(cat(v[:, sl], X[h]), KBt)

    for h in heads:
        sl = sls[h]
        mu = jnp.mean(Y[h], axis=-1, keepdims=True)
        yc = Y[h] - mu
        var = jnp.mean(yc * yc, axis=-1, keepdims=True)
        yn = yc * lax.rsqrt(var + GN_EPS) * lnw_ref[:, sl] + lnb_ref[:, sl]
        bonus = jnp.sum(r[:, sl] * k2[:, sl] * rk_ref[:, sl], axis=-1, keepdims=True) * v[:, sl]
        z_ref[:, sl] = (yn + bonus) * g_ref[:, sl]

    @pl.when(c == nchunks - 1)
    def _():
        sout_ref[0] = S_ref[...]


def _wkv(rkv, lw, a, g, k_k, k_a, r_k, lnx_w, lnx_b, state, *, B, T, C, HB, row_off):
    D = D_MODEL
    W = HB * HEAD_A
    nchunks = T // C
    base = row_off // C
    tok = lambda b, hg, c: (base + b * nchunks + c, hg)
    par = lambda b, hg, c: (0, hg)
    st = lambda b, hg, c: (b, hg, 0, 0)
    in_specs = [pl.BlockSpec((1, C, W), lambda b, hg, c, i=i: (i, base + b * nchunks + c, hg)) for i in range(3)]
    in_specs += [pl.BlockSpec((C, W), tok)] * 3
    in_specs += [pl.BlockSpec((1, W), par)] * 5
    in_specs += [pl.BlockSpec((1, HB, HEAD_A, HEAD_A), st)]
    z, s_out = pl.pallas_call(
        functools.partial(_wkv_kernel, C=C, HB=HB, nchunks=nchunks),
        grid=(B, H_A // HB, nchunks),
        in_specs=in_specs,
        out_specs=[pl.BlockSpec((C, W), lambda b, hg, c: (b * nchunks + c, hg)),
                   pl.BlockSpec((1, HB, HEAD_A, HEAD_A), st)],
        out_shape=[jax.ShapeDtypeStruct((B * T, D), F32),
                   jax.ShapeDtypeStruct((B, H_A, HEAD_A, HEAD_A), F32)],
        scratch_shapes=[pltpu.VMEM((HB, HEAD_A, HEAD_A), F32)],
        compiler_params=_params("parallel", "parallel", "arbitrary"),
        name=f"wkv7_chunk{C}",
    )(rkv, rkv, rkv, lw, a, g, k_k.reshape(1, D), k_a.reshape(1, D), r_k.reshape(1, D),
      lnx_w.reshape(1, D), lnx_b.reshape(1, D), state)
    return z, s_out


def _mm_kernel(*refs, pro, epi, scale, tail):
    it = iter(refs)
    x_ref = next(it)
    t_ref = next(it) if tail else None
    g_ref = next(it) if pro == "rms" else None
    w_ref = next(it)
    e_ref = next(it) if epi is not None else None
    o_ref = next(it)
    lhs_ref = next(it)
    is_tail = pl.program_id(0) == pl.num_programs(0) - 1

    @pl.when(pl.program_id(1) == 0)
    def _():
        x = x_ref[...]
        if tail:
            x = jnp.where(is_tail, t_ref[...], x)
        if pro == "rms":
            ms = jnp.mean(x * x, axis=-1, keepdims=True)
            x = x * lax.rsqrt(ms + NORM_EPS) * g_ref[...]
        lhs_ref[...] = x.astype(BF16)

    acc = jnp.dot(lhs_ref[...], w_ref[...].astype(BF16), preferred_element_type=F32)
    if epi == "res":
        o_ref[...] = acc + e_ref[...]
    elif epi == "logsig":
        o_ref[...] = _log_sigmoid(acc + e_ref[...])
    elif epi == "headrms":
        for h in range(acc.shape[1] // HD_B):
            sl = slice(h * HD_B, (h + 1) * HD_B)
            t = acc[:, sl]
            ms = jnp.mean(t * t, axis=-1, keepdims=True)
            o_ref[:, sl] = t * lax.rsqrt(ms + NORM_EPS) * e_ref[...] * scale
    else:
        o_ref[...] = acc


def _mm(x, w, *, tm, tn, x_tail=None, w_lead=0, w_col0=0, n_out=None, gain=None, epi=None, extra=None, scale=1.0,
        name="mm"):
    K = x.shape[1]
    M = x.shape[0] + (0 if x_tail is None else x_tail.shape[0])
    N = w.shape[-1] if n_out is None else n_out
    nm = M // tm
    pro = "rms" if gain is not None else "cast"
    args = [x]
    if x_tail is None:
        in_specs = [pl.BlockSpec((tm, K), lambda m, n: (m, 0))]
    else:
        assert x.shape[0] % tm == 0 and x_tail.shape[0] == tm
        in_specs = [pl.BlockSpec((tm, K), lambda m, n: (jnp.minimum(m, nm - 2), 0)),
                    pl.BlockSpec((tm, K), lambda m, n: (0, 0))]
        args.append(x_tail)
    if gain is not None:
        args.append(gain.reshape(1, K))
        in_specs.append(pl.BlockSpec((1, K), lambda m, n: (0, 0)))
    args.append(w)
    c0 = w_col0 // tn
    if w.ndim == 3:
        in_specs.append(pl.BlockSpec((None, K, tn), lambda m, n: (w_lead, 0, n + c0)))
    else:
        in_specs.append(pl.BlockSpec((K, tn), lambda m, n: (0, n + c0)))
    if epi == "res":
        args.append(extra)
        in_specs.append(pl.BlockSpec((tm, tn), lambda m, n: (m, n)))
    elif epi == "logsig":
        args.append(extra.reshape(1, N))
        in_specs.append(pl.BlockSpec((1, tn), lambda m, n: (0, n)))
    elif epi == "headrms":
        args.append(extra.reshape(1, HD_B))
        in_specs.append(pl.BlockSpec((1, HD_B), lambda m, n: (0, 0)))
    return pl.pallas_call(
        functools.partial(_mm_kernel, pro=pro, epi=epi, scale=scale, tail=x_tail is not None),
        grid=(nm, N // tn),
        in_specs=in_specs,
        out_specs=pl.BlockSpec((tm, tn), lambda m, n: (m, n)),
        out_shape=jax.ShapeDtypeStruct((M, N), F32),
        scratch_shapes=[pltpu.VMEM((tm, K), BF16)],
        compiler_params=_params("parallel", "arbitrary"),
        name=name,
    )(*args)


def _ffn_down_kernel(u_ref, g0_ref, halo_ref, s1_ref, s2_ref, cw_ref, cb_ref, w_ref, r_ref, o_ref, *, tm, nm, seqs):
    m = pl.program_id(0)
    kstep = pl.program_id(1)

    @pl.when(kstep == 0)
    def _():
        o_ref[...] = r_ref[...]

    g0 = g0_ref[...]
    halo = halo_ref[...]
    r_in, tpos = _token_pos(g0.shape, m, tm, seqs)
    is_last = m == nm - 1
    g1 = jnp.where(tpos >= 1, _delayed(g0, halo, 1, r_in), jnp.where(is_last, s1_ref[...], 0.0))
    g2 = jnp.where(tpos >= 2, _delayed(g0, halo, 2, r_in), jnp.where(is_last, s2_ref[...], 0.0))
    gc = cb_ref[...] + g2 * cw_ref[0:1, :]
    gc = gc + g1 * cw_ref[1:2, :]
    gc = gc + g0 * cw_ref[2:3, :]
    hid = gc * _sigmoid(gc) * u_ref[...]
    o_ref[...] += jnp.dot(hid.astype(BF16), w_ref[...], preferred_element_type=F32)


def _ffn_down(ug, conv_state, conv_w, conv_b, w_down, res, *, tm, tk, seqs):
    M = ug.shape[0]
    F, D = w_down.shape
    nk = F // tk
    nm = M // tm
    last_k = lambda m, k: (0, jnp.where(m == nm - 1, k, 0))
    return pl.pallas_call(
        functools.partial(_ffn_down_kernel, tm=tm, nm=nm, seqs=seqs),
        grid=(nm, nk),
        in_specs=[pl.BlockSpec((tm, tk), lambda m, k: (m, k)),
                  pl.BlockSpec((tm, tk), lambda m, k: (m, nk + k)),
                  pl.BlockSpec((HALO, tk), lambda m, k: (jnp.maximum(m * (tm // HALO) - 1, 0), nk + k)),
                  pl.BlockSpec((tm, tk), last_k),
                  pl.BlockSpec((tm, tk), last_k),
                  pl.BlockSpec((3, tk), lambda m, k: (0, k)),
                  pl.BlockSpec((1, tk), lambda m, k: (0, k)),
                  pl.BlockSpec((tk, D), lambda m, k: (k, 0)),
                  pl.BlockSpec((tm, D), lambda m, k: (m, 0))],
        out_specs=pl.BlockSpec((tm, D), lambda m, k: (m, 0)),
        out_shape=jax.ShapeDtypeStruct((M, D), F32),
        compiler_params=_params("parallel", "arbitrary"),
        name="ffn_down",
    )(ug, ug, ug, _last_tile_rows(conv_state[:, 1:], tm, seqs[2]), _last_tile_rows(conv_state, tm, seqs[2]),
      conv_w, conv_b.reshape(1, F), w_down, res)


def _cumsum_kernel(x_ref, o_ref, carry_ref):
    @pl.when(pl.program_id(1) == 0)
    def _():
        carry_ref[...] = jnp.zeros_like(carry_ref)

    tc = x_ref.shape[0]
    rows = lax.broadcasted_iota(jnp.int32, (tc, tc), 0)
    cols = lax.broadcasted_iota(jnp.int32, (tc, tc), 1)
    cs = _dot_sel_l((cols <= rows).astype(BF16), x_ref[...]) + carry_ref[...]
    o_ref[...] = cs
    carry_ref[...] = cs[tc - 1:tc, :]


def _cumsum_rows(x, *, B, T, tc):
    n = T // tc
    return pl.pallas_call(
        _cumsum_kernel,
        grid=(B, n),
        in_specs=[pl.BlockSpec((tc, x.shape[1]), lambda b, j: (b * n + j, 0))],
        out_specs=pl.BlockSpec((tc, x.shape[1]), lambda b, j: (b * n + j, 0)),
        out_shape=jax.ShapeDtypeStruct((B * T, x.shape[1]), F32),
        scratch_shapes=[pltpu.VMEM((1, x.shape[1]), F32)],
        compiler_params=_params("parallel", "arbitrary"),
        name="logf_cumsum",
    )(x)


def _online_update(s, v_bf, m_ref, l_ref, acc_ref, h, sl):
    m_prev = m_ref[h]
    m_new = jnp.maximum(m_prev, jnp.max(s, axis=-1, keepdims=True))
    alpha = jnp.exp(m_prev - m_new)
    p = jnp.exp(s - m_new)
    l_ref[h] = alpha * l_ref[h] + jnp.sum(p, axis=-1, keepdims=True)
    acc_ref[:, sl] = alpha * acc_ref[:, sl] + jnp.dot(p.astype(BF16), v_bf, preferred_element_type=F32)
    m_ref[h] = m_new


def _fox_prompt_kernel(q_ref, k_ref, v_ref, cq_ref, ck_ref, o_ref, m_ref, l_ref, acc_ref, *, tq, tk):
    qi = pl.program_id(1)
    kj = pl.program_id(2)
    last = (qi * tq + tq - 1) // tk

    @pl.when(kj == 0)
    def _():
        m_ref[...] = jnp.full_like(m_ref, -jnp.inf)
        l_ref[...] = jnp.zeros_like(l_ref)
        acc_ref[...] = jnp.zeros_like(acc_ref)

    def block(masked):
        if masked:
            rows = qi * tq + lax.broadcasted_iota(jnp.int32, (tq, tk), 0)
            cols = kj * tk + lax.broadcasted_iota(jnp.int32, (tq, tk), 1)
            visible = cols <= rows
        for h in range(H_B):
            sl = slice(h * HD_B, (h + 1) * HD_B)
            s = _dot_nt(q_ref[:, sl], k_ref[:, sl]) + cq_ref[:, h:h + 1] - ck_ref[0, h:h + 1, :]
            if masked:
                s = jnp.where(visible, s, -jnp.inf)
            _online_update(s, v_ref[:, sl].astype(BF16), m_ref, l_ref, acc_ref, h, sl)

    @pl.when(kj < last)
    def _():
        block(False)

    @pl.when(kj == last)
    def _():
        block(True)
        for h in range(H_B):
            sl = slice(h * HD_B, (h + 1) * HD_B)
            o_ref[:, sl] = acc_ref[:, sl] / l_ref[h]


def _fox_prompt(q, k, v, c_col, c_row, *, B, T, tq, tk):
    D = D_MODEL
    nq, nk = T // tq, T // tk
    qmap = lambda b, qi, kj: (b * nq + qi, 0)
    kblk = lambda qi, kj: jnp.minimum(kj, (qi * tq + tq - 1) // tk)
    kmap = lambda b, qi, kj: (b * nk + kblk(qi, kj), 0)
    return pl.pallas_call(
        functools.partial(_fox_prompt_kernel, tq=tq, tk=tk),
        grid=(B, nq, nk),
        in_specs=[pl.BlockSpec((tq, D), qmap), pl.BlockSpec((tk, D), kmap), pl.BlockSpec((tk, D), kmap),
                  pl.BlockSpec((tq, c_col.shape[1]), qmap),
                  pl.BlockSpec((1, H_B, tk), lambda b, qi, kj: (b, 0, kblk(qi, kj)))],
        out_specs=pl.BlockSpec((tq, D), qmap),
        out_shape=jax.ShapeDtypeStruct((B * T, D), F32),
        scratch_shapes=[pltpu.VMEM((H_B, tq, 1), F32), pltpu.VMEM((H_B, tq, 1), F32), pltpu.VMEM((tq, D), F32)],
        compiler_params=_params("parallel", "parallel", "arbitrary"),
        name="fox_prompt",
    )(q, k, v, c_col, c_row)


def _accumulate(s_tiles, v_of_head, m_ref, l_ref, acc_ref, Tn):
    m_prev = m_ref[...]
    m_new = jnp.maximum(m_prev, jnp.max(functools.reduce(jnp.maximum, s_tiles), axis=1, keepdims=True))
    alpha = jnp.exp(m_prev - m_new)
    p = [jnp.exp(t - m_new) for t in s_tiles]
    l_ref[...] = alpha * l_ref[...] + jnp.sum(functools.reduce(jnp.add, p), axis=1, keepdims=True)
    m_ref[...] = m_new
    pv = []
    for h in range(H_B):
        lhs = jnp.concatenate([t[h * Tn:(h + 1) * Tn, :] for t in p], axis=1).astype(BF16)
        pv.append(jnp.dot(lhs, v_of_head(h), preferred_element_type=F32))
    acc_ref[...] = alpha * acc_ref[...] + jnp.concatenate(pv, axis=0)


def _fox_sample_kernel(pt_ref, q_ref, lfn_ref, kn_ref, vn_ref, *rest, G, ngroups):
    k_refs, v_refs, lf_refs = rest[:G], rest[G:2 * G], rest[2 * G:3 * G]
    (o_ref, qbd_ref, qt_ref, ekey_ref, kbf_ref, vbf_ref, m_ref, l_ref, acc_ref, mrow_ref, lrow_ref, acct_ref,
     carry_ref, cq_ref, xn_ref) = rest[3 * G:]
    g = pl.program_id(1)
    Tn, D = q_ref.shape
    R = k_refs[0].shape[1]
    P = R // H_B
    HT = H_B * Tn
    tn = lambda a, b: lax.dot_general(a, b, (((0,), (0,)), ((), ())), preferred_element_type=F32)
    iota = lambda shape, d: lax.broadcasted_iota(jnp.int32, shape, d)
    hsl = lambda h: slice(h * HD_B, (h + 1) * HD_B)
    as_col = lambda row: jnp.broadcast_to(row, (HT, HT)).T[:, 0:1]

    @pl.when(g == 0)
    def _():
        q = q_ref[...]
        lane_head = iota((Tn, D), 1) // HD_B
        for hp in range(H_B // 2):
            pair = jnp.concatenate([jnp.where(lane_head == 2 * hp, q, 0.0),
                                    jnp.where(lane_head == 2 * hp + 1, q, 0.0)], axis=0)
            qbd_ref[hp * 2 * Tn:(hp + 1) * 2 * Tn, :] = pair.astype(BF16)
        qt_ref[...] = jnp.concatenate([q[:, hsl(h)] for h in range(H_B)], axis=0).T.astype(BF16)
        ekey_ref[...] = (iota((R, P), 0) // H_B == iota((R, P), 1)).astype(BF16)
        x = lfn_ref[...]
        t_idx = iota(x.shape, 0)
        cs = jnp.zeros_like(x)
        for u in range(Tn):
            cs = cs + jnp.where(t_idx >= u, x[u:u + 1, :], 0.0)
        spread = (iota((x.shape[1], HT), 0) == iota((x.shape[1], HT), 1) // Tn).astype(BF16)
        xn = _dot_sel_r(cs, spread)
        xn_ref[...] = xn
        own = iota((Tn, HT), 0) == iota((Tn, HT), 1) % Tn
        cq_ref[...] = jnp.sum(jnp.where(own, xn, 0.0), axis=0, keepdims=True)
        mrow_ref[...] = jnp.full_like(mrow_ref, -jnp.inf)
        lrow_ref[...] = jnp.zeros_like(lrow_ref)
        acct_ref[...] = jnp.zeros_like(acct_ref)
        carry_ref[...] = jnp.zeros_like(carry_ref)

    after = (iota((P, P), 0) > iota((P, P), 1)).astype(BF16)
    head_cols = (iota((H_B, HT), 0) == iota((H_B, HT), 1) // Tn).astype(BF16)
    own_head = iota((R, HT), 0) % H_B == iota((R, HT), 1) // Tn
    ekey = ekey_ref[...]
    carry = carry_ref[...]
    s_l, c_l = [], []
    for i in range(G):
        lf = lf_refs[i][0]
        suf = _dot_sel_r(lf, after) + carry
        carry = suf[:, 0:1] + lf[:, 0:1]
        hi, mid, lo = _split3(suf)
        bias = tn(hi, head_cols) + (tn(mid, head_cols) + tn(lo, head_cols))
        off = bias[P - 1:P, :]
        b_hi, b_lo = _split2(bias - off)
        lhs = jnp.concatenate([k_refs[i][0].astype(BF16), ekey, ekey], axis=1)
        rhs = jnp.concatenate([qt_ref[...], b_hi, b_lo], axis=0)
        s = jnp.dot(lhs, rhs, preferred_element_type=F32)
        s_l.append(jnp.where(own_head, s, -jnp.inf))
        c_l.append(cq_ref[...] + off)
    carry_ref[...] = carry
    m_prev = mrow_ref[...]
    m_new = functools.reduce(jnp.maximum, [jnp.max(s, axis=0, keepdims=True) + c for s, c in zip(s_l, c_l)], m_prev)
    alpha = jnp.exp(m_prev - m_new)
    p_l = [jnp.exp(s - (m_new - c)) for s, c in zip(s_l, c_l)]
    lrow_ref[...] = alpha * lrow_ref[...] + functools.reduce(jnp.add, [jnp.sum(p, axis=0, keepdims=True) for p in p_l])
    pv = [tn(v_refs[i][0].astype(BF16), p.astype(BF16)) for i, p in enumerate(p_l)]
    acct_ref[...] = alpha * acct_ref[...] + functools.reduce(jnp.add, pv)
    mrow_ref[...] = m_new

    @pl.when(g == ngroups - 1)
    def _():
        m_ref[...] = as_col(mrow_ref[...])
        l_ref[...] = as_col(lrow_ref[...])
        acc_ref[...] = acct_ref[...].T
        pad = (-Tn) % 16
        for src_ref, dst_ref in ((kn_ref, kbf_ref), (vn_ref, vbf_ref)):
            top = jnp.concatenate([src_ref[...], jnp.zeros((pad, D), F32)], axis=0)
            dst_ref[0:Tn + pad, :] = top.astype(BF16)
            dst_ref[Tn + pad:P, :] = jnp.zeros((P - Tn - pad, D), BF16)
        stn = lax.dot_general(kbf_ref[0:P, :], qbd_ref[...], (((1,), (1,)), ((), ())), preferred_element_type=F32)
        own_sum = jnp.concatenate([xn_ref[...], jnp.zeros((P - Tn, HT), F32)], axis=0)
        visible = iota((P, HT), 0) <= iota((P, HT), 1) % Tn
        stn = jnp.where(visible, stn + cq_ref[...] - own_sum, -jnp.inf)
        _accumulate([stn.T], lambda h: vbf_ref[0:P, hsl(h)], m_ref, l_ref, acc_ref, Tn)
        l = l_ref[...]
        for h in range(H_B):
            o_ref[:, hsl(h)] = acc_ref[h * Tn:(h + 1) * Tn, :] / l[h * Tn:(h + 1) * Tn, :]


def _fox_sample(page_table, q, k, v, logf, cache_k, cache_v, cache_logf_t, *, Bd, Tn, G, row_off):
    D = D_MODEL
    npages = page_table.shape[1]
    page = cache_logf_t.shape[2]
    ngroups = npages // G
    base = row_off // Tn
    tokmap = lambda b, g, pt: (base + b, 0)
    in_specs = [pl.BlockSpec((Tn, D), tokmap), pl.BlockSpec((Tn, logf.shape[1]), tokmap),
                pl.BlockSpec((Tn, D), tokmap), pl.BlockSpec((Tn, D), tokmap)]
    for _ in range(2):
        in_specs += [pl.BlockSpec((1, page * H_B, HD_B),
                                  lambda b, g, pt, i=i: (pt[b, npages - 1 - (g * G + i)], 0, 0)) for i in range(G)]
    in_specs += [pl.BlockSpec((1, H_B, page),
                              lambda b, g, pt, i=i: (pt[b, npages - 1 - (g * G + i)], 0, 0)) for i in range(G)]
    HT = H_B * Tn
    grid_spec = pltpu.PrefetchScalarGridSpec(
        num_scalar_prefetch=1,
        grid=(Bd, ngroups),
        in_specs=in_specs,
        out_specs=pl.BlockSpec((Tn, D), lambda b, g, pt: (b, 0)),
        scratch_shapes=[pltpu.VMEM((HT, D), BF16), pltpu.VMEM((HD_B, HT), BF16), pltpu.VMEM((page * H_B, page), BF16),
                        pltpu.VMEM((page, D), BF16), pltpu.VMEM((page, D), BF16),
                        pltpu.VMEM((HT, 1), F32), pltpu.VMEM((HT, 1), F32), pltpu.VMEM((HT, HD_B), F32),
                        pltpu.VMEM((1, HT), F32), pltpu.VMEM((1, HT), F32), pltpu.VMEM((HD_B, HT), F32),
                        pltpu.VMEM((H_B, 1), F32), pltpu.VMEM((1, HT), F32), pltpu.VMEM((Tn, HT), F32)],
    )
    return pl.pallas_call(
        functools.partial(_fox_sample_kernel, G=G, ngroups=ngroups),
        grid_spec=grid_spec,
        out_shape=jax.ShapeDtypeStruct((Bd * Tn, D), F32),
        compiler_params=_params("parallel", "arbitrary"),
        name="fox_sample",
    )(page_table, q, logf, k, v, *([cache_k] * G), *([cache_v] * G), *([cache_logf_t] * G))


def kernel(x_prompt, x_sample, state_wkv, state_shift, state_conv, cache_k, cache_v, cache_logf, page_table, att_norm, time_mix, w_rkv, w0, w1, w2, a0, a1, a2, g1, g2, k_k, k_a, r_k, lnx_w, lnx_b, w_o_a, ffn_norm, w_ug, conv_w, conv_b, w_down, kv_norm, w_kvf, b_f, k_norm, attn_norm_b, w_q, q_norm, w_o_b):
    D = D_MODEL
    Bp, Tp, _ = x_prompt.shape
    Bd, Td, _ = x_sample.shape
    Mp, Md = Bp * Tp, Bd * Td
    F = conv_b.shape[1]
    bf = lambda t: t.astype(BF16)

    def split(t):
        return t[:Mp].reshape(Bp, Tp, -1), t[Mp:].reshape(Bd, Td, -1)

    def join(tp, ts):
        return jnp.concatenate([tp.reshape(Mp, -1), ts.reshape(Md, -1)], axis=0)

    x = join(x_prompt, x_sample)

    seqs = (Mp, Tp, Td)

    def last_rows(t, n, c0=0):
        prompt = jnp.stack([t[(b + 1) * Tp - n:(b + 1) * Tp, c0:] for b in range(Bp)])
        return prompt, t[Mp:, c0:].reshape(Bd, Td, -1)[:, Td - n:]

    xn, xprev = _rmsnorm_shift(x, att_norm[0], state_shift[0], tm=384, seqs=seqs)
    p_shift, s_shift = last_rows(xn, 1)
    mix = time_mix[0]
    rkv = _rkv(xn, xprev, mix[jnp.array([0, 2, 3])], bf(w_rkv[0]), tm=768, tn=1024)
    pad_c = lambda t: jnp.pad(t, ((0, 0), (0, LORA_PAD - t.shape[1])))
    pad_r = lambda t: jnp.pad(t, ((0, LORA_PAD - t.shape[0]), (0, 0)))
    lw, a, g = _lora(xn, xprev, mix[jnp.array([1, 4, 5])], bf(pad_c(w1[0])), bf(pad_c(a1[0])), bf(g1[0]),
                     bf(pad_r(w2[0])), bf(pad_r(a2[0])), bf(g2[0]), w0[0], a0[0], tm=384)
    wkv_args = (rkv, lw, a, g, k_k[0], k_a[0], r_k[0], lnx_w[0], lnx_b[0])
    z_p, p_wkv = _wkv(*wkv_args, jnp.zeros((Bp, H_A, HEAD_A, HEAD_A), F32), B=Bp, T=Tp, C=64, HB=16, row_off=0)
    z_s, s_wkv = _wkv(*wkv_args, state_wkv[0], B=Bd, T=Td, C=Td, HB=16, row_off=Mp)
    x = _mm(z_p, w_o_a, tm=Md, tn=D, x_tail=z_s, epi="res", extra=x, name="wkv_out_proj")

    w_ug_bf = bf(w_ug)

    def conv_ffn(x, layer, conv0_s):
        ug = _mm(x, w_ug_bf, w_lead=layer, tm=768, tn=1024, gain=ffn_norm[layer], name="ffn_up")
        y = _ffn_down(ug, conv0_s, conv_w[layer], conv_b[layer], bf(w_down[layer]), x, tm=384, tk=1408, seqs=seqs)
        return (y,) + last_rows(ug, 2, F)

    x, p_conv0, s_conv0 = conv_ffn(x, 0, state_conv[0])

    w_kvf_bf = bf(w_kvf)
    k_all = _mm(x, w_kvf_bf, n_out=D, tm=768, tn=1024, gain=kv_norm, epi="headrms", extra=k_norm, name="k_proj")
    v_all = _mm(x, w_kvf_bf, w_col0=D, n_out=D, tm=768, tn=1024, gain=kv_norm, name="v_proj")
    w_f = jnp.pad(w_kvf[:, 2 * D:], ((0, 0), (0, HD_B - H_B)))
    logf_all = _mm(x, bf(w_f), tm=768, tn=HD_B, gain=kv_norm, epi="logsig", extra=jnp.pad(b_f, (0, HD_B - H_B)),
                   name="logf_proj")

    q_all = _mm(x, bf(w_q), tm=768, tn=1024, gain=attn_norm_b[0], epi="headrms", extra=q_norm[0],
                scale=ATTN_SCALE, name="q_proj")
    c_col = _cumsum_rows(logf_all, B=Bp, T=Tp, tc=512)
    c_row = jnp.transpose(c_col.reshape(Bp, Tp, HD_B)[:, :, :H_B], (0, 2, 1))
    o_p = _fox_prompt(q_all, k_all, v_all, c_col, c_row, B=Bp, T=Tp, tq=256, tk=512)

    pool_rows = lambda t: t.reshape(t.shape[0], t.shape[1] * H_B, HD_B)
    o_s = _fox_sample(page_table, q_all, k_all, v_all, logf_all, pool_rows(cache_k), pool_rows(cache_v),
                      jnp.transpose(cache_logf, (0, 2, 1)), Bd=Bd, Tn=Td, G=8, row_off=Mp)
    x = _mm(o_p, w_o_b, tm=Md, tn=D, x_tail=o_s, epi="res", extra=x, name="attn_out_proj")

    x, p_conv1, s_conv1 = conv_ffn(x, 1, state_conv[1])

    y_p, y_s = split(x)
    k_p, k_s = split(k_all)
    v_p, v_s = split(v_all)
    lf_p, lf_s = split(logf_all[:, :H_B])
    heads = lambda t: t.reshape(t.shape[0], t.shape[1], H_B, HD_B)
    return (y_p, y_s,
            p_wkv[None], p_shift[:, 0][None], jnp.stack([p_conv0, p_conv1]), heads(k_p), heads(v_p), lf_p,
            s_wkv[None], s_shift[:, 0][None], jnp.stack([s_conv0, s_conv1]), heads(k_s), heads(v_s), lf_s)
```

```python
import functools
import math

import jax
import jax.numpy as jnp
from jax import lax
from jax.experimental import pallas as pl
from jax.experimental.pallas import tpu as pltpu

F32 = jnp.float32
BF16 = jnp.bfloat16

D_MODEL = 2048
HEAD_A = 64
H_A = D_MODEL // HEAD_A
HD_B = 128
H_B = D_MODEL // HD_B
NORM_EPS = 1e-6
GN_EPS = 64e-5
ATTN_SCALE = HD_B ** -0.5
LORA_PAD = 128
VMEM_LIMIT = 56 * 1024 * 1024


def _params(*sem):
    return pltpu.CompilerParams(dimension_semantics=sem, vmem_limit_bytes=VMEM_LIMIT)


def _sigmoid(x):
    return 1.0 / (1.0 + jnp.exp(-x))


def _log_sigmoid(z):
    return jnp.minimum(z, 0.0) - jnp.log1p(jnp.exp(-jnp.abs(z)))


def _dot(a, b):
    return jnp.dot(a.astype(BF16), b.astype(BF16), preferred_element_type=F32)


def _dot_nt(a, b):
    return lax.dot_general(a.astype(BF16), b.astype(BF16), (((1,), (1,)), ((), ())), preferred_element_type=F32)


def _dot_tn(a, b):
    return lax.dot_general(a.astype(BF16), b.astype(BF16), (((0,), (0,)), ((), ())), preferred_element_type=F32)


def _split2(x):
    hi = x.astype(BF16)
    return hi, (x - hi.astype(F32)).astype(BF16)


def _split3(x):
    hi = x.astype(BF16)
    r1 = x - hi.astype(F32)
    mid = r1.astype(BF16)
    return hi, mid, (r1 - mid.astype(F32)).astype(BF16)


def _dot3(a, b):
    ah, al = _split2(a)
    bh, bl = _split2(b)
    d = lambda x, y: jnp.dot(x, y, preferred_element_type=F32)
    return d(ah, bh) + (d(ah, bl) + d(al, bh))


def _dot_sel_l(sel, x):
    hi, mid, lo = _split3(x)
    d = lambda p: jnp.dot(sel, p, preferred_element_type=F32)
    return d(hi) + (d(mid) + d(lo))


def _dot_sel_r(x, sel):
    hi, mid, lo = _split3(x)
    d = lambda p: jnp.dot(p, sel, preferred_element_type=F32)
    return d(hi) + (d(mid) + d(lo))


HALO = 8


def _token_pos(shape, m, tm, seqs):
    Mp, Tp, Td = seqs
    r_in = lax.broadcasted_iota(jnp.int32, shape, 0)
    g = m * tm + r_in
    return r_in, jnp.where(g < Mp, g % Tp, (g - Mp) % Td)


def _delayed(x, halo, n, r_in):
    out = pltpu.roll(x, n, 0)
    for j in range(n):
        out = jnp.where(r_in == j, halo[HALO - n + j:HALO - n + j + 1, :], out)
    return out


def _rms_kernel(x_ref, halo_ref, st_ref, g_ref, o_ref, prev_ref, *, tm, nm, seqs):
    m = pl.program_id(0)

    def norm(x):
        ms = jnp.mean(x * x, axis=-1, keepdims=True)
        return x * lax.rsqrt(ms + NORM_EPS) * g_ref[...]

    xn = norm(x_ref[...])
    o_ref[...] = xn
    r_in, tpos = _token_pos(xn.shape, m, tm, seqs)
    first = jnp.where(m == nm - 1, st_ref[...], 0.0)
    prev_ref[...] = jnp.where(tpos >= 1, _delayed(xn, norm(halo_ref[...]), 1, r_in), first)


def _last_tile_rows(state, tm, Td):
    Bd, n, W = state.shape
    rows = jnp.pad(state, ((0, 0), (0, Td - n), (0, 0))).reshape(Bd * Td, W)
    return jnp.pad(rows, ((tm - Bd * Td, 0), (0, 0)))


def _rmsnorm_shift(x, g, shift_state, *, tm, seqs):
    M, D = x.shape
    nm = M // tm
    out = jax.ShapeDtypeStruct((M, D), F32)
    return pl.pallas_call(
        functools.partial(_rms_kernel, tm=tm, nm=nm, seqs=seqs),
        grid=(nm,),
        in_specs=[pl.BlockSpec((tm, D), lambda i: (i, 0)),
                  pl.BlockSpec((HALO, D), lambda i: (jnp.maximum(i * (tm // HALO) - 1, 0), 0)),
                  pl.BlockSpec((tm, D), lambda i: (0, 0)),
                  pl.BlockSpec((1, D), lambda i: (0, 0))],
        out_specs=[pl.BlockSpec((tm, D), lambda i: (i, 0))] * 2,
        out_shape=[out, out],
        compiler_params=_params("parallel"),
        name="rmsnorm_shift",
    )(x, x, _last_tile_rows(shift_state[:, None], tm, seqs[2]), g.reshape(1, D))


def _rkv_kernel(xn_ref, xp_ref, mix_ref, w_ref, o_ref, lhs_ref):
    @pl.when(pl.program_id(2) == 0)
    def _():
        xn = xn_ref[...]
        lhs_ref[...] = (xn + (xp_ref[...] - xn) * mix_ref[0]).astype(BF16)

    o_ref[0] = jnp.dot(lhs_ref[...], w_ref[0].astype(BF16), preferred_element_type=F32)


def _rkv(xn, xprev, mix3, w3, tm, tn):
    M, D = xn.shape
    N = w3.shape[2]
    return pl.pallas_call(
        _rkv_kernel,
        grid=(M // tm, 3, N // tn),
        in_specs=[pl.BlockSpec((tm, D), lambda m, c, n: (m, 0)),
                  pl.BlockSpec((tm, D), lambda m, c, n: (m, 0)),
                  pl.BlockSpec((1, 1, D), lambda m, c, n: (c, 0, 0)),
                  pl.BlockSpec((1, D, tn), lambda m, c, n: (c, 0, n))],
        out_specs=pl.BlockSpec((1, tm, tn), lambda m, c, n: (c, m, n)),
        out_shape=jax.ShapeDtypeStruct((3, M, N), F32),
        scratch_shapes=[pltpu.VMEM((tm, D), BF16)],
        compiler_params=_params("parallel", "arbitrary", "arbitrary"),
        name="rkv_proj",
    )(xn, xprev, mix3.reshape(3, 1, D), w3)


def _lora_kernel(xn_ref, xp_ref, mix_ref, w1_ref, a1_ref, g1_ref, w2_ref, a2_ref, g2_ref, w0_ref, a0_ref,
                 lw_ref, a_ref, g_ref):
    xn = xn_ref[...]
    xx = xp_ref[...] - xn
    xw = xn + xx * mix_ref[0:1, :]
    xa = xn + xx * mix_ref[1:2, :]
    xg = xn + xx * mix_ref[2:3, :]
    hw = jnp.tanh(_dot(xw, w1_ref[...]))
    wv = w0_ref[...] + _dot(hw, w2_ref[...])
    w_raw = _log_sigmoid(wv) - 0.5
    lw_ref[...] = -jnp.exp(w_raw)
    ha = _dot(xa, a1_ref[...])
    a_ref[...] = _sigmoid(a0_ref[...] + _dot(ha, a2_ref[...]))
    hg = _sigmoid(_dot(xg, g1_ref[...]))
    g_ref[...] = _dot(hg, g2_ref[...])


def _lora(xn, xprev, mix3, w1, a1, g1, w2, a2, g2, w0, a0, tm):
    M, D = xn.shape
    row = lambda m: (m, 0)
    full = lambda m: (0, 0)
    specs = [pl.BlockSpec((tm, D), row), pl.BlockSpec((tm, D), row), pl.BlockSpec((3, D), full)]
    specs += [pl.BlockSpec(t.shape, full) for t in (w1, a1, g1, w2, a2, g2)]
    specs += [pl.BlockSpec((1, D), full), pl.BlockSpec((1, D), full)]
    out = jax.ShapeDtypeStruct((M, D), F32)
    return pl.pallas_call(
        _lora_kernel,
        grid=(M // tm,),
        in_specs=specs,
        out_specs=[pl.BlockSpec((tm, D), row)] * 3,
        out_shape=[out, out, out],
        compiler_params=_params("parallel"),
        name="lora_branches",
    )(xn, xprev, mix3, w1, a1, g1, w2, a2, g2, w0.reshape(1, D), a0.reshape(1, D))


def _wkv_kernel(r_ref, k_ref, v_ref, lw_ref, a_ref, g_ref, kk_ref, ka_ref, rk_ref, lnw_ref, lnb_ref, s0_ref,
                z_ref, sout_ref, S_ref, *, C, HB, nchunks):
    c = pl.program_id(2)

    @pl.when(c == 0)
    def _():
        S_ref[...] = s0_ref[0]

    rows = lax.broadcasted_iota(jnp.int32, (C, C), 0)
    cols = lax.broadcasted_iota(jnp.int32, (C, C), 1)
    incl = cols <= rows
    strict = cols < rows
    eye = (rows == cols).astype(F32)

    lw = lw_ref[...]
    cum = _dot_sel_l(incl.astype(BF16), lw)
    cum_last = cum[C - 1:C, :]
    e_pos = jnp.exp(cum)
    e_neg = jnp.exp(-cum)
    e_prev = jnp.exp(cum - lw)
    e_tail = jnp.exp(cum_last - cum)
    g_last = jnp.exp(cum_last)

    r = r_ref[0]
    k = k_ref[0]
    v = v_ref[0]
    a = a_ref[...]
    kkf = k * kk_ref[...]
    k2 = k * (1.0 + (a - 1.0) * ka_ref[...])

    heads = range(HB)
    sls = [slice(h * HEAD_A, (h + 1) * HEAD_A) for h in heads]
    cat = lambda x, y: jnp.concatenate([x, y], axis=0)

    kk_l, b_l = [], []
    for sl in sls:
        kk_h = kkf[:, sl]
        nrm = jnp.sqrt(jnp.sum(kk_h * kk_h, axis=-1, keepdims=True))
        kk_h = kk_h / jnp.maximum(nrm, 1e-12)
        kk_l.append(kk_h)
        b_l.append(kk_h * a[:, sl])
    QR = [cat(kk_l[h] * e_prev[:, sls[h]], r[:, sls[h]] * e_pos[:, sls[h]]) for h in heads]
    Kd = [k2[:, sl] * e_neg[:, sl] for sl in sls]
    Bd = [b_l[h] * e_neg[:, sls[h]] for h in heads]
    sk = [_dot_nt(QR[h], Kd[h]) for h in heads]
    sb = [_dot_nt(QR[h], Bd[h]) for h in heads]
    LU = [cat(jnp.where(strict, sk[h][:C], 0.0), jnp.where(incl, sk[h][C:], 0.0)) for h in heads]
    Ub = [jnp.where(incl, sb[h][C:], 0.0) for h in heads]

    Q = [-jnp.where(strict, sb[h][:C], 0.0) for h in heads]
    T = [eye + Q[h] for h in heads]
    levels = int(math.log2(C))
    if levels > 1:
        Q = [_dot3(Q[h], Q[h]) for h in heads]
    for lvl in range(1, levels):
        if lvl < levels - 1:
            R = [_dot3(cat(T[h], Q[h]), Q[h]) for h in heads]
            T = [T[h] + R[h][:C] for h in heads]
            Q = [R[h][C:] for h in heads]
        else:
            T = [T[h] + _dot3(T[h], Q[h]) for h in heads]

    S = [S_ref[h] for h in heads]
    SQ = [_dot_nt(QR[h], S[h]) for h in heads]
    LV = [_dot(LU[h], v[:, sls[h]]) for h in heads]
    X = [_dot3(T[h], SQ[h][:C] + LV[h][:C]) for h in heads]
    Y = [SQ[h][C:] + LV[h][C:] - _dot(Ub[h], X[h]) for h in heads]
    for h in heads:
        sl = sls[h]
        KBt = cat(k2[:, sl] * e_tail[:, sl], -(b_l[h] * e_tail[:, sl]))
        S_ref[h] = S[h] * g_last[:, sl] + _dot_tn(cat(v[:, sl], X[h]), KBt)

    for h in heads:
        sl = sls[h]
        mu = jnp.mean(Y[h], axis=-1, keepdims=True)
        yc = Y[h] - mu
        var = jnp.mean(yc * yc, axis=-1, keepdims=True)
        yn = yc * lax.rsqrt(var + GN_EPS) * lnw_ref[:, sl] + lnb_ref[:, sl]
        bonus = jnp.sum(r[:, sl] * k2[:, sl] * rk_ref[:, sl], axis=-1, keepdims=True) * v[:, sl]
        z_ref[:, sl] = (yn + bonus) * g_ref[:, sl]

    @pl.when(c == nchunks - 1)
    def _():
        sout_ref[0] = S_ref[...]


def _wkv(rkv, lw, a, g, k_k, k_a, r_k, lnx_w, lnx_b, state, *, B, T, C, HB, row_off):
    D = D_MODEL
    W = HB * HEAD_A
    nchunks = T // C
    base = row_off // C
    tok = lambda b, hg, c: (base + b * nchunks + c, hg)
    par = lambda b, hg, c: (0, hg)
    st = lambda b, hg, c: (b, hg, 0, 0)
    in_specs = [pl.BlockSpec((1, C, W), lambda b, hg, c, i=i: (i, base + b * nchunks + c, hg)) for i in range(3)]
    in_specs += [pl.BlockSpec((C, W), tok)] * 3
    in_specs += [pl.BlockSpec((1, W), par)] * 5
    in_specs += [pl.BlockSpec((1, HB, HEAD_A, HEAD_A), st)]
    z, s_out = pl.pallas_call(
        functools.partial(_wkv_kernel, C=C, HB=HB, nchunks=nchunks),
        grid=(B, H_A // HB, nchunks),
        in_specs=in_specs,
        out_specs=[pl.BlockSpec((C, W), lambda b, hg, c: (b * nchunks + c, hg)),
                   pl.BlockSpec((1, HB, HEAD_A, HEAD_A), st)],
        out_shape=[jax.ShapeDtypeStruct((B * T, D), F32),
                   jax.ShapeDtypeStruct((B, H_A, HEAD_A, HEAD_A), F32)],
        scratch_shapes=[pltpu.VMEM((HB, HEAD_A, HEAD_A), F32)],
        compiler_params=_params("parallel", "parallel", "arbitrary"),
        name=f"wkv7_chunk{C}",
    )(rkv, rkv, rkv, lw, a, g, k_k.reshape(1, D), k_a.reshape(1, D), r_k.reshape(1, D),
      lnx_w.reshape(1, D), lnx_b.reshape(1, D), state)
    return z, s_out


def _mm_kernel(*refs, pro, epi, scale, tail):
    it = iter(refs)
    x_ref = next(it)
    t_ref = next(it) if tail else None
    g_ref = next(it) if pro == "rms" else None
    w_ref = next(it)
    e_ref = next(it) if epi is not None else None
    o_ref = next(it)
    lhs_ref = next(it)
    is_tail = pl.program_id(0) == pl.num_programs(0) - 1

    @pl.when(pl.program_id(1) == 0)
    def _():
        x = x_ref[...]
        if tail:
            x = jnp.where(is_tail, t_ref[...], x)
        if pro == "rms":
            ms = jnp.mean(x * x, axis=-1, keepdims=True)
            x = x * lax.rsqrt(ms + NORM_EPS) * g_ref[...]
        lhs_ref[...] = x.astype(BF16)

    acc = jnp.dot(lhs_ref[...], w_ref[...].astype(BF16), preferred_element_type=F32)
    if epi == "res":
        o_ref[...] = acc + e_ref[...]
    elif epi == "logsig":
        o_ref[...] = _log_sigmoid(acc + e_ref[...])
    elif epi == "headrms":
        for h in range(acc.shape[1] // HD_B):
            sl = slice(h * HD_B, (h + 1) * HD_B)
            t = acc[:, sl]
            ms = jnp.mean(t * t, axis=-1, keepdims=True)
            o_ref[:, sl] = t * lax.rsqrt(ms + NORM_EPS) * e_ref[...] * scale
    else:
        o_ref[...] = acc


def _mm(x, w, *, tm, tn, x_tail=None, w_lead=0, w_col0=0, n_out=None, gain=None, epi=None, extra=None, scale=1.0,
        name="mm"):
    K = x.shape[1]
    M = x.shape[0] + (0 if x_tail is None else x_tail.shape[0])
    N = w.shape[-1] if n_out is None else n_out
    nm = M // tm
    pro = "rms" if gain is not None else "cast"
    args = [x]
    if x_tail is None:
        in_specs = [pl.BlockSpec((tm, K), lambda m, n: (m, 0))]
    else:
        assert x.shape[0] % tm == 0 and x_tail.shape[0] == tm
        in_specs = [pl.BlockSpec((tm, K), lambda m, n: (jnp.minimum(m, nm - 2), 0)),
                    pl.BlockSpec((tm, K), lambda m, n: (0, 0))]
        args.append(x_tail)
    if gain is not None:
        args.append(gain.reshape(1, K))
        in_specs.append(pl.BlockSpec((1, K), lambda m, n: (0, 0)))
    args.append(w)
    c0 = w_col0 // tn
    if w.ndim == 3:
        in_specs.append(pl.BlockSpec((None, K, tn), lambda m, n: (w_lead, 0, n + c0)))
    else:
        in_specs.append(pl.BlockSpec((K, tn), lambda m, n: (0, n + c0)))
    if epi == "res":
        args.append(extra)
        in_specs.append(pl.BlockSpec((tm, tn), lambda m, n: (m, n)))
    elif epi == "logsig":
        args.append(extra.reshape(1, N))
        in_specs.append(pl.BlockSpec((1, tn), lambda m, n: (0, n)))
    elif epi == "headrms":
        args.append(extra.reshape(1, HD_B))
        in_specs.append(pl.BlockSpec((1, HD_B), lambda m, n: (0, 0)))
    return pl.pallas_call(
        functools.partial(_mm_kernel, pro=pro, epi=epi, scale=scale, tail=x_tail is not None),
        grid=(nm, N // tn),
        in_specs=in_specs,
        out_specs=pl.BlockSpec((tm, tn), lambda m, n: (m, n)),
        out_shape=jax.ShapeDtypeStruct((M, N), F32),
        scratch_shapes=[pltpu.VMEM((tm, K), BF16)],
        compiler_params=_params("parallel", "arbitrary"),
        name=name,
    )(*args)


def _ffn_down_kernel(u_ref, g0_ref, halo_ref, s1_ref, s2_ref, cw_ref, cb_ref, w_ref, r_ref, o_ref, *, tm, nm, seqs):
    m = pl.program_id(0)
    kstep = pl.program_id(1)

    @pl.when(kstep == 0)
    def _():
        o_ref[...] = r_ref[...]

    g0 = g0_ref[...]
    halo = halo_ref[...]
    r_in, tpos = _token_pos(g0.shape, m, tm, seqs)
    is_last = m == nm - 1
    g1 = jnp.where(tpos >= 1, _delayed(g0, halo, 1, r_in), jnp.where(is_last, s1_ref[...], 0.0))
    g2 = jnp.where(tpos >= 2, _delayed(g0, halo, 2, r_in), jnp.where(is_last, s2_ref[...], 0.0))
    gc = cb_ref[...] + g2 * cw_ref[0:1, :]
    gc = gc + g1 * cw_ref[1:2, :]
    gc = gc + g0 * cw_ref[2:3, :]
    hid = gc * _sigmoid(gc) * u_ref[...]
    o_ref[...] += jnp.dot(hid.astype(BF16), w_ref[...], preferred_element_type=F32)


def _ffn_down(ug, conv_state, conv_w, conv_b, w_down, res, *, tm, tk, seqs):
    M = ug.shape[0]
    F, D = w_down.shape
    nk = F // tk
    nm = M // tm
    last_k = lambda m, k: (0, jnp.where(m == nm - 1, k, 0))
    return pl.pallas_call(
        functools.partial(_ffn_down_kernel, tm=tm, nm=nm, seqs=seqs),
        grid=(nm, nk),
        in_specs=[pl.BlockSpec((tm, tk), lambda m, k: (m, k)),
                  pl.BlockSpec((tm, tk), lambda m, k: (m, nk + k)),
                  pl.BlockSpec((HALO, tk), lambda m, k: (jnp.maximum(m * (tm // HALO) - 1, 0), nk + k)),
                  pl.BlockSpec((tm, tk), last_k),
                  pl.BlockSpec((tm, tk), last_k),
                  pl.BlockSpec((3, tk), lambda m, k: (0, k)),
                  pl.BlockSpec((1, tk), lambda m, k: (0, k)),
                  pl.BlockSpec((tk, D), lambda m, k: (k, 0)),
                  pl.BlockSpec((tm, D), lambda m, k: (m, 0))],
        out_specs=pl.BlockSpec((tm, D), lambda m, k: (m, 0)),
        out_shape=jax.ShapeDtypeStruct((M, D), F32),
        compiler_params=_params("parallel", "arbitrary"),
        name="ffn_down",
    )(ug, ug, ug, _last_tile_rows(conv_state[:, 1:], tm, seqs[2]), _last_tile_rows(conv_state, tm, seqs[2]),
      conv_w, conv_b.reshape(1, F), w_down, res)


def _cumsum_kernel(x_ref, o_ref, carry_ref):
    @pl.when(pl.program_id(1) == 0)
    def _():
        carry_ref[...] = jnp.zeros_like(carry_ref)

    tc = x_ref.shape[0]
    rows = lax.broadcasted_iota(jnp.int32, (tc, tc), 0)
    cols = lax.broadcasted_iota(jnp.int32, (tc, tc), 1)
    cs = _dot_sel_l((cols <= rows).astype(BF16), x_ref[...]) + carry_ref[...]
    o_ref[...] = cs
    carry_ref[...] = cs[tc - 1:tc, :]


def _cumsum_rows(x, *, B, T, tc):
    n = T // tc
    return pl.pallas_call(
        _cumsum_kernel,
        grid=(B, n),
        in_specs=[pl.BlockSpec((tc, x.shape[1]), lambda b, j: (b * n + j, 0))],
        out_specs=pl.BlockSpec((tc, x.shape[1]), lambda b, j: (b * n + j, 0)),
        out_shape=jax.ShapeDtypeStruct((B * T, x.shape[1]), F32),
        scratch_shapes=[pltpu.VMEM((1, x.shape[1]), F32)],
        compiler_params=_params("parallel", "arbitrary"),
        name="logf_cumsum",
    )(x)


def _online_update(s, v_bf, m_ref, l_ref, acc_ref, h, sl):
    m_prev = m_ref[h]
    m_new = jnp.maximum(m_prev, jnp.max(s, axis=-1, keepdims=True))
    alpha = jnp.exp(m_prev - m_new)
    p = jnp.exp(s - m_new)
    l_ref[h] = alpha * l_ref[h] + jnp.sum(p, axis=-1, keepdims=True)
    acc_ref[:, sl] = alpha * acc_ref[:, sl] + jnp.dot(p.astype(BF16), v_bf, preferred_element_type=F32)
    m_ref[h] = m_new


def _fox_prompt_kernel(q_ref, k_ref, v_ref, cq_ref, ck_ref, o_ref, m_ref, l_ref, acc_ref, *, tq, tk):
    qi = pl.program_id(1)
    kj = pl.program_id(2)
    last = (qi * tq + tq - 1) // tk

    @pl.when(kj == 0)
    def _():
        m_ref[...] = jnp.full_like(m_ref, -jnp.inf)
        l_ref[...] = jnp.zeros_like(l_ref)
        acc_ref[...] = jnp.zeros_like(acc_ref)

    def block(masked):
        if masked:
            rows = qi * tq + lax.broadcasted_iota(jnp.int32, (tq, tk), 0)
            cols = kj * tk + lax.broadcasted_iota(jnp.int32, (tq, tk), 1)
            visible = cols <= rows
        for h in range(H_B):
            sl = slice(h * HD_B, (h + 1) * HD_B)
            s = _dot_nt(q_ref[:, sl], k_ref[:, sl]) + cq_ref[:, h:h + 1] - ck_ref[0, h:h + 1, :]
            if masked:
                s = jnp.where(visible, s, -jnp.inf)
            _online_update(s, v_ref[:, sl].astype(BF16), m_ref, l_ref, acc_ref, h, sl)

    @pl.when(kj < last)
    def _():
        block(False)

    @pl.when(kj == last)
    def _():
        block(True)
        for h in range(H_B):
            sl = slice(h * HD_B, (h + 1) * HD_B)
            o_ref[:, sl] = acc_ref[:, sl] / l_ref[h]


def _fox_prompt(q, k, v, c_col, c_row, *, B, T, tq, tk):
    D = D_MODEL
    nq, nk = T // tq, T // tk
    qmap = lambda b, qi, kj: (b * nq + qi, 0)
    kblk = lambda qi, kj: jnp.minimum(kj, (qi * tq + tq - 1) // tk)
    kmap = lambda b, qi, kj: (b * nk + kblk(qi, kj), 0)
    return pl.pallas_call(
        functools.partial(_fox_prompt_kernel, tq=tq, tk=tk),
        grid=(B, nq, nk),
        in_specs=[pl.BlockSpec((tq, D), qmap), pl.BlockSpec((tk, D), kmap), pl.BlockSpec((tk, D), kmap),
                  pl.BlockSpec((tq, c_col.shape[1]), qmap),
                  pl.BlockSpec((1, H_B, tk), lambda b, qi, kj: (b, 0, kblk(qi, kj)))],
        out_specs=pl.BlockSpec((tq, D), qmap),
        out_shape=jax.ShapeDtypeStruct((B * T, D), F32),
        scratch_shapes=[pltpu.VMEM((H_B, tq, 1), F32), pltpu.VMEM((H_B, tq, 1), F32), pltpu.VMEM((tq, D), F32)],
        compiler_params=_params("parallel", "parallel", "arbitrary"),
        name="fox_prompt",
    )(q, k, v, c_col, c_row)


def _accumulate(s_tiles, v_of_head, m_ref, l_ref, acc_ref, Tn):
    m_prev = m_ref[...]
    m_new = jnp.maximum(m_prev, jnp.max(functools.reduce(jnp.maximum, s_tiles), axis=1, keepdims=True))
    alpha = jnp.exp(m_prev - m_new)
    p = [jnp.exp(t - m_new) for t in s_tiles]
    l_ref[...] = alpha * l_ref[...] + jnp.sum(functools.reduce(jnp.add, p), axis=1, keepdims=True)
    m_ref[...] = m_new
    pv = []
    for h in range(H_B):
        lhs = jnp.concatenate([t[h * Tn:(h + 1) * Tn, :] for t in p], axis=1).astype(BF16)
        pv.append(jnp.dot(lhs, v_of_head(h), preferred_element_type=F32))
    acc_ref[...] = alpha * acc_ref[...] + jnp.concatenate(pv, axis=0)


def _fox_sample_kernel(pt_ref, q_ref, lfn_ref, kn_ref, vn_ref, *rest, G, ngroups):
    k_refs, v_refs, lf_refs = rest[:G], rest[G:2 * G], rest[2 * G:3 * G]
    (o_ref, qbd_ref, qt_ref, ekey_ref, kbf_ref, vbf_ref, m_ref, l_ref, acc_ref, mrow_ref, lrow_ref, acct_ref,
     carry_ref, cq_ref, xn_ref) = rest[3 * G:]
    g = pl.program_id(1)
    Tn, D = q_ref.shape
    R = k_refs[0].shape[1]
    P = R // H_B
    HT = H_B * Tn
    tn = lambda a, b: lax.dot_general(a, b, (((0,), (0,)), ((), ())), preferred_element_type=F32)
    iota = lambda shape, d: lax.broadcasted_iota(jnp.int32, shape, d)
    hsl = lambda h: slice(h * HD_B, (h + 1) * HD_B)
    as_col = lambda row: jnp.broadcast_to(row, (HT, HT)).T[:, 0:1]

    @pl.when(g == 0)
    def _():
        q = q_ref[...]
        lane_head = iota((Tn, D), 1) // HD_B
        for hp in range(H_B // 2):
            pair = jnp.concatenate([jnp.where(lane_head == 2 * hp, q, 0.0),
                                    jnp.where(lane_head == 2 * hp + 1, q, 0.0)], axis=0)
            qbd_ref[hp * 2 * Tn:(hp + 1) * 2 * Tn, :] = pair.astype(BF16)
        qt_ref[...] = jnp.concatenate([q[:, hsl(h)] for h in range(H_B)], axis=0).T.astype(BF16)
        ekey_ref[...] = (iota((R, P), 0) // H_B == iota((R, P), 1)).astype(BF16)
        x = lfn_ref[...]
        t_idx = iota(x.shape, 0)
        cs = jnp.zeros_like(x)
        for u in range(Tn):
            cs = cs + jnp.where(t_idx >= u, x[u:u + 1, :], 0.0)
        spread = (iota((x.shape[1], HT), 0) == iota((x.shape[1], HT), 1) // Tn).astype(BF16)
        xn = _dot_sel_r(cs, spread)
        xn_ref[...] = xn
        own = iota((Tn, HT), 0) == iota((Tn, HT), 1) % Tn
        cq_ref[...] = jnp.sum(jnp.where(own, xn, 0.0), axis=0, keepdims=True)
        mrow_ref[...] = jnp.full_like(mrow_ref, -jnp.inf)
        lrow_ref[...] = jnp.zeros_like(lrow_ref)
        acct_ref[...] = jnp.zeros_like(acct_ref)
        carry_ref[...] = jnp.zeros_like(carry_ref)

    after = (iota((P, P), 0) > iota((P, P), 1)).astype(BF16)
    head_cols = (iota((H_B, HT), 0) == iota((H_B, HT), 1) // Tn).astype(BF16)
    own_head = iota((R, HT), 0) % H_B == iota((R, HT), 1) // Tn
    ekey = ekey_ref[...]
    carry = carry_ref[...]
    s_l, c_l = [], []
    for i in range(G):
        lf = lf_refs[i][0]
        suf = _dot_sel_r(lf, after) + carry
        carry = suf[:, 0:1] + lf[:, 0:1]
        hi, mid, lo = _split3(suf)
        bias = tn(hi, head_cols) + (tn(mid, head_cols) + tn(lo, head_cols))
        off = bias[P - 1:P, :]
        b_hi, b_lo = _split2(bias - off)
        lhs = jnp.concatenate([k_refs[i][0].astype(BF16), ekey, ekey], axis=1)
        rhs = jnp.concatenate([qt_ref[...], b_hi, b_lo], axis=0)
        s = jnp.dot(lhs, rhs, preferred_element_type=F32)
        s_l.append(jnp.where(own_head, s, -jnp.inf))
        c_l.append(cq_ref[...] + off)
    carry_ref[...] = carry
    m_prev = mrow_ref[...]
    m_new = functools.reduce(jnp.maximum, [jnp.max(s, axis=0, keepdims=True) + c for s, c in zip(s_l, c_l)], m_prev)
    alpha = jnp.exp(m_prev - m_new)
    p_l = [jnp.exp(s - (m_new - c)) for s, c in zip(s_l, c_l)]
    lrow_ref[...] = alpha * lrow_ref[...] + functools.reduce(jnp.add, [jnp.sum(p, axis=0, keepdims=True) for p in p_l])
    pv = [tn(v_refs[i][0].astype(BF16), p.astype(BF16)) for i, p in enumerate(p_l)]
    acct_ref[...] = alpha * acct_ref[...] + functools.reduce(jnp.add, pv)
    mrow_ref[...] = m_new

    @pl.when(g == ngroups - 1)
    def _():
        m_ref[...] = as_col(mrow_ref[...])
        l_ref[...] = as_col(lrow_ref[...])
        acc_ref[...] = acct_ref[...].T
        pad = (-Tn) % 16
        for src_ref, dst_ref in ((kn_ref, kbf_ref), (vn_ref, vbf_ref)):
            top = jnp.concatenate([src_ref[...], jnp.zeros((pad, D), F32)], axis=0)
            dst_ref[0:Tn + pad, :] = top.astype(BF16)
            dst_ref[Tn + pad:P, :] = jnp.zeros((P - Tn - pad, D), BF16)
        stn = lax.dot_general(kbf_ref[0:P, :], qbd_ref[...], (((1,), (1,)), ((), ())), preferred_element_type=F32)
        own_sum = jnp.concatenate([xn_ref[...], jnp.zeros((P - Tn, HT), F32)], axis=0)
        visible = iota((P, HT), 0) <= iota((P, HT), 1) % Tn
        stn = jnp.where(visible, stn + cq_ref[...] - own_sum, -jnp.inf)
        _accumulate([stn.T], lambda h: vbf_ref[0:P, hsl(h)], m_ref, l_ref, acc_ref, Tn)
        l = l_ref[...]
        for h in range(H_B):
            o_ref[:, hsl(h)] = acc_ref[h * Tn:(h + 1) * Tn, :] / l[h * Tn:(h + 1) * Tn, :]


def _fox_sample(page_table, q, k, v, logf, cache_k, cache_v, cache_logf_t, *, Bd, Tn, G, row_off):
    D = D_MODEL
    npages = page_table.shape[1]
    page = cache_logf_t.shape[2]
    ngroups = npages // G
    base = row_off // Tn
    tokmap = lambda b, g, pt: (base + b, 0)
    in_specs = [pl.BlockSpec((Tn, D), tokmap), pl.BlockSpec((Tn, logf.shape[1]), tokmap),
                pl.BlockSpec((Tn, D), tokmap), pl.BlockSpec((Tn, D), tokmap)]
    for _ in range(2):
        in_specs += [pl.BlockSpec((1, page * H_B, HD_B),
                                  lambda b, g, pt, i=i: (pt[b, npages - 1 - (g * G + i)], 0, 0)) for i in range(G)]
    in_specs += [pl.BlockSpec((1, H_B, page),
                              lambda b, g, pt, i=i: (pt[b, npages - 1 - (g * G + i)], 0, 0)) for i in range(G)]
    HT = H_B * Tn
    grid_spec = pltpu.PrefetchScalarGridSpec(
        num_scalar_prefetch=1,
        grid=(Bd, ngroups),
        in_specs=in_specs,
        out_specs=pl.BlockSpec((Tn, D), lambda b, g, pt: (b, 0)),
        scratch_shapes=[pltpu.VMEM((HT, D), BF16), pltpu.VMEM((HD_B, HT), BF16), pltpu.VMEM((page * H_B, page), BF16),
                        pltpu.VMEM((page, D), BF16), pltpu.VMEM((page, D), BF16),
                        pltpu.VMEM((HT, 1), F32), pltpu.VMEM((HT, 1), F32), pltpu.VMEM((HT, HD_B), F32),
                        pltpu.VMEM((1, HT), F32), pltpu.VMEM((1, HT), F32), pltpu.VMEM((HD_B, HT), F32),
                        pltpu.VMEM((H_B, 1), F32), pltpu.VMEM((1, HT), F32), pltpu.VMEM((Tn, HT), F32)],
    )
    return pl.pallas_call(
        functools.partial(_fox_sample_kernel, G=G, ngroups=ngroups),
        grid_spec=grid_spec,
        out_shape=jax.ShapeDtypeStruct((Bd * Tn, D), F32),
        compiler_params=_params("parallel", "arbitrary"),
        name="fox_sample",
    )(page_table, q, logf, k, v, *([cache_k] * G), *([cache_v] * G), *([cache_logf_t] * G))


def kernel(x_prompt, x_sample, state_wkv, state_shift, state_conv, cache_k, cache_v, cache_logf, page_table, att_norm, time_mix, w_rkv, w0, w1, w2, a0, a1, a2, g1, g2, k_k, k_a, r_k, lnx_w, lnx_b, w_o_a, ffn_norm, w_ug, conv_w, conv_b, w_down, kv_norm, w_kvf, b_f, k_norm, attn_norm_b, w_q, q_norm, w_o_b):
    D = D_MODEL
    Bp, Tp, _ = x_prompt.shape
    Bd, Td, _ = x_sample.shape
    Mp, Md = Bp * Tp, Bd * Td
    F = conv_b.shape[1]
    bf = lambda t: t.astype(BF16)

    def split(t):
        return t[:Mp].reshape(Bp, Tp, -1), t[Mp:].reshape(Bd, Td, -1)

    def join(tp, ts):
        return jnp.concatenate([tp.reshape(Mp, -1), ts.reshape(Md, -1)], axis=0)

    x = join(x_prompt, x_sample)

    seqs = (Mp, Tp, Td)

    def last_rows(t, n, c0=0):
        prompt = jnp.stack([t[(b + 1) * Tp - n:(b + 1) * Tp, c0:] for b in range(Bp)])
        return prompt, t[Mp:, c0:].reshape(Bd, Td, -1)[:, Td - n:]

    xn, xprev = _rmsnorm_shift(x, att_norm[0], state_shift[0], tm=384, seqs=seqs)
    p_shift, s_shift = last_rows(xn, 1)
    mix = time_mix[0]
    rkv = _rkv(xn, xprev, mix[jnp.array([0, 2, 3])], bf(w_rkv[0]), tm=768, tn=1024)
    pad_c = lambda t: jnp.pad(t, ((0, 0), (0, LORA_PAD - t.shape[1])))
    pad_r = lambda t: jnp.pad(t, ((0, LORA_PAD - t.shape[0]), (0, 0)))
    lw, a, g = _lora(xn, xprev, mix[jnp.array([1, 4, 5])], bf(pad_c(w1[0])), bf(pad_c(a1[0])), bf(g1[0]),
                     bf(pad_r(w2[0])), bf(pad_r(a2[0])), bf(g2[0]), w0[0], a0[0], tm=384)
    wkv_args = (rkv, lw, a, g, k_k[0], k_a[0], r_k[0], lnx_w[0], lnx_b[0])
    z_p, p_wkv = _wkv(*wkv_args, jnp.zeros((Bp, H_A, HEAD_A, HEAD_A), F32), B=Bp, T=Tp, C=64, HB=32, row_off=0)
    z_s, s_wkv = _wkv(*wkv_args, state_wkv[0], B=Bd, T=Td, C=Td, HB=32, row_off=Mp)
    x = _mm(z_p, w_o_a, tm=Md, tn=D, x_tail=z_s, epi="res", extra=x, name="wkv_out_proj")

    w_ug_bf = bf(w_ug)

    def conv_ffn(x, layer, conv0_s):
        ug = _mm(x, w_ug_bf, w_lead=layer, tm=768, tn=1024, gain=ffn_norm[layer], name="ffn_up")
        y = _ffn_down(ug, conv0_s, conv_w[layer], conv_b[layer], bf(w_down[layer]), x, tm=384, tk=1408, seqs=seqs)
        return (y,) + last_rows(ug, 2, F)

    x, p_conv0, s_conv0 = conv_ffn(x, 0, state_conv[0])

    w_kvf_bf = bf(w_kvf)
    k_all = _mm(x, w_kvf_bf, n_out=D, tm=768, tn=1024, gain=kv_norm, epi="headrms", extra=k_norm, name="k_proj")
    v_all = _mm(x, w_kvf_bf, w_col0=D, n_out=D, tm=768, tn=1024, gain=kv_norm, name="v_proj")
    w_f = jnp.pad(w_kvf[:, 2 * D:], ((0, 0), (0, HD_B - H_B)))
    logf_all = _mm(x, bf(w_f), tm=768, tn=HD_B, gain=kv_norm, epi="logsig", extra=jnp.pad(b_f, (0, HD_B - H_B)),
                   name="logf_proj")

    q_all = _mm(x, bf(w_q), tm=768, tn=1024, gain=attn_norm_b[0], epi="headrms", extra=q_norm[0],
                scale=ATTN_SCALE, name="q_proj")
    c_col = _cumsum_rows(logf_all, B=Bp, T=Tp, tc=512)
    c_row = jnp.transpose(c_col.reshape(Bp, Tp, HD_B)[:, :, :H_B], (0, 2, 1))
    o_p = _fox_prompt(q_all, k_all, v_all, c_col, c_row, B=Bp, T=Tp, tq=256, tk=512)

    pool_rows = lambda t: t.reshape(t.shape[0], t.shape[1] * H_B, HD_B)
    o_s = _fox_sample(page_table, q_all, k_all, v_all, logf_all, pool_rows(cache_k), pool_rows(cache_v),
                      jnp.transpose(cache_logf, (0, 2, 1)), Bd=Bd, Tn=Td, G=8, row_off=Mp)
    x = _mm(o_p, w_o_b, tm=Md, tn=D, x_tail=o_s, epi="res", extra=x, name="attn_out_proj")

    x, p_conv1, s_conv1 = conv_ffn(x, 1, state_conv[1])

    y_p, y_s = split(x)
    k_p, k_s = split(k_all)
    v_p, v_s = split(v_all)
    lf_p, lf_s = split(logf_all[:, :H_B])
    heads = lambda t: t.reshape(t.shape[0], t.shape[1], H_B, HD_B)
    return (y_p, y_s,
            p_wkv[None], p_shift[:, 0][None], jnp.stack([p_conv0, p_conv1]), heads(k_p), heads(v_p), lf_p,
            s_wkv[None], s_shift[:, 0][None], jnp.stack([s_conv0, s_conv1]), heads(k_s), heads(v_s), lf_s)
```
